```python
import jax, jax.numpy as jnp
from jax import lax
import numpy as np

D_MODEL = 1024
BATCH = 8
SEQ = 8192
DEPTH = 4

HEAD_DIM = 64
SB_WIDTH = D_MODEL // 2
N_SB_HEADS = SB_WIDTH // HEAD_DIM
SG_WIDTH = D_MODEL // 2
SG_GROUP_DIM = 64
SG_GROUPS = SG_WIDTH // SG_GROUP_DIM
MIX_WIDTH = SB_WIDTH + SG_WIDTH
IN_WIDTH = 3 * SB_WIDTH + 2 * SG_WIDTH
CHUNK = 128
Q_BLOCK = 128
CONV_K = 31
CONV_WIDTH = D_MODEL
D_FF = ((8 * D_MODEL // 3 + 127) // 128) * 128
FFN_K = 3
N_EVEN = (DEPTH + 1) // 2
N_ODD = DEPTH // 2
EPS = 1e-6

kernel_name = "stickbreak_sgu_conformer_convglu_hybrid"


def rms_norm(x, g):
    xf = x.astype(jnp.float32)
    y = xf * lax.rsqrt(jnp.mean(xf * xf, axis=-1, keepdims=True) + EPS)
    return (y * g.astype(jnp.float32)).astype(x.dtype)


def layer_norm(x, g, b):
    xf = x.astype(jnp.float32)
    mu = jnp.mean(xf, axis=-1, keepdims=True)
    xc = xf - mu
    y = xc * lax.rsqrt(jnp.mean(xc * xc, axis=-1, keepdims=True) + EPS)
    return (y * g.astype(jnp.float32) + b.astype(jnp.float32)).astype(x.dtype)


def causal_depthwise_conv(x, w, b):
    k, c = w.shape
    y = lax.conv_general_dilated(
        x, w[:, None, :].astype(x.dtype), window_strides=(1,),
        padding=[(k - 1, 0)], dimension_numbers=("NWC", "WIO", "NWC"),
        feature_group_count=c)
    return y + b.astype(x.dtype)


def stick_breaking_attention(q, k, v):
    b_, s_, h_, dh = q.shape
    n_blk = s_ // Q_BLOCK
    scale = dh ** -0.5
    qb = q.reshape(b_, n_blk, Q_BLOCK, h_, dh).transpose(1, 0, 3, 2, 4)
    kt = k.transpose(0, 2, 1, 3).astype(jnp.float32)
    vt = v.transpose(0, 2, 1, 3).astype(jnp.float32)
    key_pos = jnp.arange(s_)

    def one_block(args):
        q_blk, blk = args
        z = jnp.einsum("bhqd,bhkd->bhqk", q_blk.astype(jnp.float32), kt) * scale
        q_pos = blk * Q_BLOCK + jnp.arange(Q_BLOCK)
        causal = key_pos[None, :] < q_pos[:, None]
        log_keep = jnp.where(causal, jax.nn.log_sigmoid(-z), 0.0)
        rev = lax.cumsum(log_keep, axis=3, reverse=True)
        tail = jnp.concatenate([rev[..., 1:], jnp.zeros_like(rev[..., :1])], axis=-1)
        w = jnp.where(causal, jnp.exp(jax.nn.log_sigmoid(z) + tail), 0.0)
        return jnp.einsum("bhqk,bhkd->bhqd", w, vt)

    out = lax.map(one_block, (qb, jnp.arange(n_blk)))
    return out.transpose(1, 0, 3, 2, 4).reshape(b_, s_, h_ * dh).astype(q.dtype)


def spatial_gating(u, z, g_z, w_s, b_s):
    b_, s_, _ = u.shape
    u = jax.nn.gelu(u, approximate=False)
    z = jax.nn.gelu(z, approximate=False)
    zg = z.reshape(b_, s_, SG_GROUPS, SG_GROUP_DIM)
    zf = zg.astype(jnp.float32)
    zg = (zf * lax.rsqrt(jnp.mean(zf * zf, axis=-1, keepdims=True) + EPS)
          * g_z.reshape(SG_GROUPS, SG_GROUP_DIM).astype(jnp.float32)).astype(z.dtype)
    zc = zg.reshape(b_, s_ // CHUNK, CHUNK, SG_GROUPS, SG_GROUP_DIM)
    mask = jnp.tril(jnp.ones((CHUNK, CHUNK), dtype=bool))
    wm = jnp.where(mask[None], w_s, 0.0).astype(z.dtype)
    s = jnp.einsum("gts,bcsgd->bctgd", wm, zc) + b_s.T[:, :, None].astype(z.dtype)
    return u * s.reshape(b_, s_, SG_WIDTH)


def attn_sgu_mixer(h, w_in, q_g, k_g, z_g, w_s, b_s, w_out):
    b_, s_, _ = h.shape
    proj = h @ w_in
    q, k, v, u, z = jnp.split(
        proj, [SB_WIDTH, 2 * SB_WIDTH, 3 * SB_WIDTH, 3 * SB_WIDTH + SG_WIDTH], axis=-1)
    q = rms_norm(q.reshape(b_, s_, N_SB_HEADS, HEAD_DIM), q_g)
    k = rms_norm(k.reshape(b_, s_, N_SB_HEADS, HEAD_DIM), k_g)
    v = v.reshape(b_, s_, N_SB_HEADS, HEAD_DIM)
    a = stick_breaking_attention(q, k, v)
    g = spatial_gating(u, z, z_g, w_s, b_s)
    return jnp.concatenate([a, g], axis=-1) @ w_out


def conformer_conv(h, w1, b1, w_dw, b_dw, ln_g, ln_b, w2, b2):
    a, gate = jnp.split(h @ w1 + b1, 2, axis=-1)
    y = a * jax.nn.sigmoid(gate)
    y = causal_depthwise_conv(y, w_dw, b_dw)
    y = jax.nn.silu(layer_norm(y, ln_g, ln_b))
    return y @ w2 + b2


def conv_glu_ffn(h, w_up, w_dw, b_dw, w_down):
    gate, val = jnp.split(h @ w_up, 2, axis=-1)
    gate = causal_depthwise_conv(gate, w_dw, b_dw)
    return (jax.nn.silu(gate) * val) @ w_down


def setup_inputs(seed: int = 0) -> dict:
    key = jax.random.key(seed)
    ks = jax.random.split(key, 24)
    f32 = jnp.float32

    def nrm(k, shape, scale):
        return jax.random.normal(k, shape, f32) * scale

    def gain(k, shape):
        return 1.0 + 0.02 * jax.random.normal(k, shape, f32)

    return {
        "x": jax.random.normal(ks[0], (BATCH, SEQ, D_MODEL), f32),
        "mix_norm_g": gain(ks[1], (DEPTH, D_MODEL)),
        "sb_w_in": nrm(ks[2], (N_EVEN, D_MODEL, IN_WIDTH), D_MODEL ** -0.5),
        "sb_q_norm_g": gain(ks[3], (N_EVEN, HEAD_DIM)),
        "sb_k_norm_g": gain(ks[4], (N_EVEN, HEAD_DIM)),
        "sg_z_norm_g": gain(ks[5], (N_EVEN, SG_WIDTH)),
        "sg_w_spatial": nrm(ks[6], (N_EVEN, SG_GROUPS, CHUNK, CHUNK), CHUNK ** -0.5),
        "sg_b_spatial": gain(ks[7], (N_EVEN, SG_GROUPS, CHUNK)),
        "hyb_w_out": nrm(ks[8], (N_EVEN, MIX_WIDTH, D_MODEL), MIX_WIDTH ** -0.5),
        "cv_w_pw1": nrm(ks[9], (N_ODD, D_MODEL, 2 * CONV_WIDTH), D_MODEL ** -0.5),
        "cv_b_pw1": nrm(ks[10], (N_ODD, 2 * CONV_WIDTH), 0.02),
        "cv_w_dw": nrm(ks[11], (N_ODD, CONV_K, CONV_WIDTH), CONV_K ** -0.5),
        "cv_b_dw": nrm(ks[12], (N_ODD, CONV_WIDTH), 0.02),
        "cv_ln_g": gain(ks[13], (N_ODD, CONV_WIDTH)),
        "cv_ln_b": nrm(ks[14], (N_ODD, CONV_WIDTH), 0.02),
        "cv_w_pw2": nrm(ks[15], (N_ODD, CONV_WIDTH, D_MODEL), CONV_WIDTH ** -0.5),
        "cv_b_pw2": nrm(ks[16], (N_ODD, D_MODEL), 0.02),
        "ffn_norm_g": gain(ks[17], (DEPTH, D_MODEL)),
        "ffn_w_up": nrm(ks[18], (DEPTH, D_MODEL, 2 * D_FF), D_MODEL ** -0.5),
        "ffn_w_dw": nrm(ks[19], (DEPTH, FFN_K, D_FF), FFN_K ** -0.5),
        "ffn_b_dw": nrm(ks[20], (DEPTH, D_FF), 0.02),
        "ffn_w_down": nrm(ks[21], (DEPTH, D_FF, D_MODEL), D_FF ** -0.5),
    }


def reference(x, mix_norm_g, sb_w_in, sb_q_norm_g, sb_k_norm_g, sg_z_norm_g,
              sg_w_spatial, sg_b_spatial, hyb_w_out, cv_w_pw1, cv_b_pw1, cv_w_dw,
              cv_b_dw, cv_ln_g, cv_ln_b, cv_w_pw2, cv_b_pw2, ffn_norm_g, ffn_w_up,
              ffn_w_dw, ffn_b_dw, ffn_w_down):
    for i in range(DEPTH):
        h = rms_norm(x, mix_norm_g[i])
        j = i // 2
        if i % 2 == 0:
            x = x + attn_sgu_mixer(h, sb_w_in[j], sb_q_norm_g[j], sb_k_norm_g[j],
                                   sg_z_norm_g[j], sg_w_spatial[j], sg_b_spatial[j],
                                   hyb_w_out[j])
        else:
            x = x + conformer_conv(h, cv_w_pw1[j], cv_b_pw1[j], cv_w_dw[j], cv_b_dw[j],
                                   cv_ln_g[j], cv_ln_b[j], cv_w_pw2[j], cv_b_pw2[j])
        h = rms_norm(x, ffn_norm_g[i])
        x = x + conv_glu_ffn(h, ffn_w_up[i], ffn_w_dw[i], ffn_b_dw[i], ffn_w_down[i])
    return x
```

```python
import functools

import jax
import jax.numpy as jnp
import numpy as np
from jax import lax
from jax.experimental import pallas as pl
from jax.experimental.pallas import tpu as pltpu

F32 = jnp.float32
BF16 = jnp.bfloat16
EPS = 1e-6

HEAD_DIM = 64
N_HEADS = 8
SB_WIDTH = HEAD_DIM * N_HEADS
SG_GROUP = 64
SG_GROUPS = 8
SG_WIDTH = SG_GROUP * SG_GROUPS
CHUNK = 128
CONV_K = 31
FFN_K = 3

LANES = 128
TOK_TILE = 512
Q_BLOCK = 256
K_BLOCK = 256
FF_CHUNK = 256
CV_CHUNK = 256
CV_ROWS = 32
CV_HALO = 32
FF_HALO = 8
VMEM_LIMIT = 56 * 1024 * 1024


def _dot(a, b):
    return jnp.dot(a, b, preferred_element_type=F32)


def _rms_norm(x, g):
    return x * lax.rsqrt(jnp.mean(x * x, axis=-1, keepdims=True) + EPS) * g


def _gelu(t):
    return 0.5 * t * (1.0 + lax.erf(t * np.float32(0.7071067811865476)))


def _split_dot(t, m):
    hi = t.astype(BF16)
    lo = (t - hi.astype(F32)).astype(BF16)
    return _dot(hi, m) + _dot(lo, m)


def _const_spec(shape):
    nd = len(shape)
    return pl.BlockSpec(shape, lambda *_: (0,) * nd, pipeline_mode=pl.Buffered(1))


def _params(*sem):
    return pltpu.CompilerParams(dimension_semantics=sem, vmem_limit_bytes=VMEM_LIMIT)


def _mixer_in_kernel(x_ref, ng_ref, w_ref, qg_ref, kg_ref, zg_ref, bd_ref, ws_ref, bs_ref,
                     q_out, kt_out, v_out, g_out):
    tt = x_ref.shape[0]
    h = _rms_norm(x_ref[...], ng_ref[...]).astype(BF16)
    proj = _dot(h, w_ref[...])
    q = proj[:, 0:SB_WIDTH]
    k = proj[:, SB_WIDTH:2 * SB_WIDTH]
    v = proj[:, 2 * SB_WIDTH:3 * SB_WIDTH]
    u = proj[:, 3 * SB_WIDTH:3 * SB_WIDTH + SG_WIDTH]
    z = proj[:, 3 * SB_WIDTH + SG_WIDTH:]
    bd = bd_ref[...]

    def group_norm(t, g):
        ms = _split_dot(t * t, bd) * np.float32(1.0 / HEAD_DIM)
        return t * lax.rsqrt(ms + EPS) * g

    q_out[...] = (group_norm(q, qg_ref[...]) * np.float32(HEAD_DIM ** -0.5)).astype(BF16)
    kn = group_norm(k, kg_ref[...])
    for j in range(tt // K_BLOCK):
        kt_out[j] = kn[j * K_BLOCK:(j + 1) * K_BLOCK, :].T.astype(BF16)
    v_out[...] = v.astype(BF16)

    ug = _gelu(u)
    zn = group_norm(_gelu(z), zg_ref[...]).astype(BF16)
    row = lax.broadcasted_iota(jnp.int32, (CHUNK, CHUNK), 0)
    col = lax.broadcasted_iota(jnp.int32, (CHUNK, CHUNK), 1)
    lane = lax.broadcasted_iota(jnp.int32, (CHUNK, LANES), 1)
    wm = [jnp.where(col <= row, ws_ref[g], 0.0).astype(BF16) for g in range(SG_GROUPS)]
    bs = bs_ref[...]
    for c in range(tt // CHUNK):
        rows = slice(c * CHUNK, (c + 1) * CHUNK)
        for p in range(SG_WIDTH // LANES):
            cols = slice(p * LANES, (p + 1) * LANES)
            zp = zn[rows, cols]
            s = jnp.where(lane < SG_GROUP, _dot(wm[2 * p], zp), _dot(wm[2 * p + 1], zp))
            g_out[rows, cols] = (ug[rows, cols] * (s + bs[:, cols])).astype(BF16)


def _mixer_in(x2, ng, w_in, qg, kg, zg, bd, ws, bs):
    n, d = x2.shape
    tt = TOK_TILE
    in_w = w_in.shape[1]
    return pl.pallas_call(
        _mixer_in_kernel,
        grid=(n // tt,),
        in_specs=[
            pl.BlockSpec((tt, d), lambda i: (i, 0)),
            _const_spec((1, d)),
            _const_spec((d, in_w)),
            _const_spec((1, SB_WIDTH)),
            _const_spec((1, SB_WIDTH)),
            _const_spec((1, SG_WIDTH)),
            _const_spec((SB_WIDTH, SB_WIDTH)),
            _const_spec((SG_GROUPS, CHUNK, CHUNK)),
            _const_spec((CHUNK, SG_WIDTH)),
        ],
        out_specs=[
            pl.BlockSpec((tt, SB_WIDTH), lambda i: (i, 0)),
            pl.BlockSpec((tt // K_BLOCK, SB_WIDTH, K_BLOCK), lambda i: (i, 0, 0)),
            pl.BlockSpec((tt, SB_WIDTH), lambda i: (i, 0)),
            pl.BlockSpec((tt, SG_WIDTH), lambda i: (i, 0)),
        ],
        out_shape=[
            jax.ShapeDtypeStruct((n, SB_WIDTH), BF16),
            jax.ShapeDtypeStruct((n // K_BLOCK, SB_WIDTH, K_BLOCK), BF16),
            jax.ShapeDtypeStruct((n, SB_WIDTH), BF16),
            jax.ShapeDtypeStruct((n, SG_WIDTH), BF16),
        ],
        compiler_params=_params("parallel"),
        name="mixer_in",
    )(x2, ng, w_in, qg, kg, zg, bd, ws, bs)


def _attn_kernel(q_ref, kt_ref, v_ref, tri_ref, o_ref, qm_ref, acc_ref, car_ref):
    i = pl.program_id(1)
    lane = lax.broadcasted_iota(jnp.int32, (Q_BLOCK, LANES), 1)
    row = lax.broadcasted_iota(jnp.int32, (Q_BLOCK, K_BLOCK), 0)
    col = lax.broadcasted_iota(jnp.int32, (Q_BLOCK, K_BLOCK), 1)
    causal = col < row
    low = lane < HEAD_DIM

    for h in range(N_HEADS):
        qp = q_ref[:, (h // 2) * LANES:(h // 2 + 1) * LANES]
        keep = low if h % 2 == 0 else jnp.logical_not(low)
        qm_ref[h] = jnp.where(keep, qp, jnp.zeros_like(qp))
    acc_ref[...] = jnp.zeros_like(acc_ref)
    car_ref[...] = jnp.zeros_like(car_ref)

    def block(kb, diag):
        k0 = pl.multiple_of(kb * K_BLOCK, K_BLOCK)
        for h in range(N_HEADS):
            grp = slice((h // 2) * LANES, (h // 2 + 1) * LANES)
            z = _dot(qm_ref[h], kt_ref[kb, grp, :])
            sp = jnp.log(1.0 + jnp.exp(-jnp.abs(z)))
            ls = jnp.minimum(z, 0.0) - sp
            lk = ls - z
            if diag:
                lk = jnp.where(causal, lk, 0.0)
            car = car_ref[h]
            tail = _split_dot(lk, tri_ref[...]) + jnp.concatenate([car, car], axis=1)
            w = jnp.exp(ls + tail)
            if diag:
                w = jnp.where(causal, w, 0.0)
            acc_ref[h] += _dot(w.astype(BF16), v_ref[pl.ds(k0, K_BLOCK), grp])
            car_ref[h] = car + jnp.sum(lk, axis=1, keepdims=True)

    block(i, True)

    def body(j, carry):
        block(i - 1 - j, False)
        return carry

    lax.fori_loop(0, i, body, 0)

    for p in range(N_HEADS // 2):
        o_ref[:, p * LANES:(p + 1) * LANES] = jnp.where(
            low, acc_ref[2 * p], acc_ref[2 * p + 1]).astype(o_ref.dtype)


def _attention(q, kt, v, tri, batch, seq):
    nq = seq // Q_BLOCK
    nk = seq // K_BLOCK
    return pl.pallas_call(
        _attn_kernel,
        grid=(batch, nq),
        in_specs=[
            pl.BlockSpec((Q_BLOCK, SB_WIDTH), lambda b, i: (b * nq + i, 0)),
            pl.BlockSpec((nk, SB_WIDTH, K_BLOCK), lambda b, i: (b, 0, 0), pipeline_mode=pl.Buffered(1)),
            pl.BlockSpec((seq, SB_WIDTH), lambda b, i: (b, 0), pipeline_mode=pl.Buffered(1)),
            _const_spec((K_BLOCK, K_BLOCK)),
        ],
        out_specs=pl.BlockSpec((Q_BLOCK, SB_WIDTH), lambda b, i: (b * nq + i, 0)),
        out_shape=jax.ShapeDtypeStruct((batch * seq, SB_WIDTH), BF16),
        scratch_shapes=[
            pltpu.VMEM((N_HEADS, Q_BLOCK, LANES), BF16),
            pltpu.VMEM((N_HEADS, Q_BLOCK, LANES), F32),
            pltpu.VMEM((N_HEADS, Q_BLOCK, LANES), F32),
        ],
        compiler_params=_params("parallel", "parallel"),
        name="sb_attention",
    )(q, kt, v, tri)


def _mixer_out_kernel(x_ref, a_ref, g_ref, wa_ref, wg_ref, o_ref):
    o_ref[...] = x_ref[...] + _dot(a_ref[...], wa_ref[...]) + _dot(g_ref[...], wg_ref[...])


def _mixer_out(x2, a, g, wa, wg):
    n, d = x2.shape
    tt = TOK_TILE
    return pl.pallas_call(
        _mixer_out_kernel,
        grid=(n // tt,),
        in_specs=[
            pl.BlockSpec((tt, d), lambda i: (i, 0)),
            pl.BlockSpec((tt, SB_WIDTH), lambda i: (i, 0)),
            pl.BlockSpec((tt, SG_WIDTH), lambda i: (i, 0)),
            _const_spec((SB_WIDTH, d)),
            _const_spec((SG_WIDTH, d)),
        ],
        out_specs=pl.BlockSpec((tt, d), lambda i: (i, 0)),
        out_shape=jax.ShapeDtypeStruct((n, d), F32),
        compiler_params=_params("parallel"),
        name="mixer_out",
    )(x2, a, g, wa, wg)


def _ffn_kernel(tiles_per_seq, x_ref, ng_ref, wg_ref, wv_ref, wdw_ref, bdw_ref, wd_ref, o_ref, cbuf_ref):
    t = pl.program_id(0)
    tt = x_ref.shape[0]
    n_chunks = wg_ref.shape[0]

    @pl.when(t == 0)
    def _():
        cbuf_ref[...] = jnp.zeros_like(cbuf_ref)

    first = (t % tiles_per_seq) == 0
    x = x_ref[...]
    h = _rms_norm(x, ng_ref[...]).astype(BF16)
    acc = x
    for c in range(n_chunks):
        gate = _dot(h, wg_ref[c])
        val = _dot(h, wv_ref[c])
        prev = jnp.where(first, 0.0, cbuf_ref[c, tt:tt + FF_HALO, :])
        cbuf_ref[c, 0:FF_HALO, :] = prev
        cbuf_ref[c, FF_HALO:FF_HALO + tt, :] = gate
        wk = wdw_ref[c]
        conv = bdw_ref[c] + gate * wk[FFN_K - 1:FFN_K, :]
        for k in range(FFN_K - 1):
            off = FF_HALO - (FFN_K - 1) + k
            conv = conv + cbuf_ref[c, off:off + tt, :] * wk[k:k + 1, :]
        act = conv * jax.nn.sigmoid(conv) * val
        acc = acc + _dot(act.astype(BF16), wd_ref[c])
    o_ref[...] = acc


def _ffn(x2, ng, wg, wv, wdw, bdw, wd, seq):
    n, d = x2.shape
    tt = TOK_TILE
    nc = wg.shape[0]
    return pl.pallas_call(
        functools.partial(_ffn_kernel, seq // tt),
        grid=(n // tt,),
        in_specs=[
            pl.BlockSpec((tt, d), lambda i: (i, 0)),
            _const_spec((1, d)),
            _const_spec((nc, d, FF_CHUNK)),
            _const_spec((nc, d, FF_CHUNK)),
            _const_spec((nc, 8, FF_CHUNK)),
            _const_spec((nc, 1, FF_CHUNK)),
            _const_spec((nc, FF_CHUNK, d)),
        ],
        out_specs=pl.BlockSpec((tt, d), lambda i: (i, 0)),
        out_shape=jax.ShapeDtypeStruct((n, d), F32),
        scratch_shapes=[pltpu.VMEM((nc, tt + FF_HALO, FF_CHUNK), F32)],
        compiler_params=_params("arbitrary"),
        name="conv_glu_ffn",
    )(x2, ng, wg, wv, wdw, bdw, wd)


def _conformer_kernel(tiles_per_seq, x_ref, ng_ref, w1_ref, b1_ref, wdw_ref, bdw_ref, lng_ref, lnb_ref,
                      w2_ref, b2_ref, o_ref, ybuf_ref, cv_ref):
    t = pl.program_id(0)
    tt = x_ref.shape[0]
    cw = w1_ref.shape[1] // 2
    n_chunks = cw // CV_CHUNK

    @pl.when(t == 0)
    def _():
        ybuf_ref[...] = jnp.zeros_like(ybuf_ref)

    first = (t % tiles_per_seq) == 0
    x = x_ref[...]
    h = _rms_norm(x, ng_ref[...]).astype(BF16)
    pre = _dot(h, w1_ref[...]) + b1_ref[...]
    y = pre[:, :cw] * jax.nn.sigmoid(pre[:, cw:])
    for c in range(n_chunks):
        prev = jnp.where(first, 0.0, ybuf_ref[c, tt:tt + CV_HALO, :])
        ybuf_ref[c, CV_HALO:CV_HALO + tt, :] = y[:, c * CV_CHUNK:(c + 1) * CV_CHUNK]
        ybuf_ref[c, 0:CV_HALO, :] = prev

    def conv_chunk(c, carry):
        bias = bdw_ref[c]
        for rb in range(tt // CV_ROWS):
            r0 = rb * CV_ROWS
            acc = jnp.broadcast_to(bias, (CV_ROWS, CV_CHUNK))
            for k in range(CONV_K):
                off = CV_HALO - (CONV_K - 1) + k + r0
                acc = acc + ybuf_ref[c, off:off + CV_ROWS, :] * wdw_ref[c, k:k + 1, :]
            cv_ref[c, r0:r0 + CV_ROWS, :] = acc
        return carry

    lax.fori_loop(0, n_chunks, conv_chunk, 0)

    s1 = cv_ref[0].sum(axis=-1, keepdims=True)
    for c in range(1, n_chunks):
        s1 = s1 + cv_ref[c].sum(axis=-1, keepdims=True)
    mu = s1 * np.float32(1.0 / cw)
    s2 = jnp.zeros_like(mu)
    for c in range(n_chunks):
        dc = cv_ref[c] - mu
        s2 = s2 + (dc * dc).sum(axis=-1, keepdims=True)
    rstd = lax.rsqrt(s2 * np.float32(1.0 / cw) + EPS)
    out = x + b2_ref[...]
    for c in range(n_chunks):
        ln = (cv_ref[c] - mu) * rstd * lng_ref[c] + lnb_ref[c]
        a = ln * jax.nn.sigmoid(ln)
        out = out + _dot(a.astype(BF16), w2_ref[c])
    o_ref[...] = out


def _conformer(x2, ng, w1, b1, wdw, bdw, lng, lnb, w2, b2, seq):
    n, d = x2.shape
    tt = TOK_TILE
    cw = w1.shape[1] // 2
    nc = cw // CV_CHUNK
    return pl.pallas_call(
        functools.partial(_conformer_kernel, seq // tt),
        grid=(n // tt,),
        in_specs=[
            pl.BlockSpec((tt, d), lambda i: (i, 0)),
            _const_spec((1, d)),
            _const_spec((d, 2 * cw)),
            _const_spec((1, 2 * cw)),
            _const_spec((nc, 32, CV_CHUNK)),
            _const_spec((nc, 1, CV_CHUNK)),
            _const_spec((nc, 1, CV_CHUNK)),
            _const_spec((nc, 1, CV_CHUNK)),
            _const_spec((nc, CV_CHUNK, d)),
            _const_spec((1, d)),
        ],
        out_specs=pl.BlockSpec((tt, d), lambda i: (i, 0)),
        out_shape=jax.ShapeDtypeStruct((n, d), F32),
        scratch_shapes=[
            pltpu.VMEM((nc, tt + CV_HALO, CV_CHUNK), F32),
            pltpu.VMEM((nc, tt, CV_CHUNK), F32),
        ],
        compiler_params=_params("arbitrary"),
        name="conformer",
    )(x2, ng, w1, b1, wdw, bdw, lng, lnb, w2, b2)


def _chunk_cols(w, width):
    lead = w.shape[:-1]
    n = w.shape[-1] // width
    return jnp.moveaxis(w.reshape(lead + (n, width)), -2, 0)


def _pad_rows(w, rows):
    return jnp.pad(w, ((0, 0), (0, rows - w.shape[1]), (0, 0)))


def kernel(x, mix_norm_g, sb_w_in, sb_q_norm_g, sb_k_norm_g, sg_z_norm_g, sg_w_spatial, sg_b_spatial,
           hyb_w_out, cv_w_pw1, cv_b_pw1, cv_w_dw, cv_b_dw, cv_ln_g, cv_ln_b, cv_w_pw2, cv_b_pw2,
           ffn_norm_g, ffn_w_up, ffn_w_dw, ffn_b_dw, ffn_w_down):
    batch, seq, d = x.shape
    depth = mix_norm_g.shape[0]
    d_ff = ffn_w_down.shape[1]
    x2 = x.reshape(batch * seq, d)

    gid = np.arange(SB_WIDTH) // HEAD_DIM
    bd = jnp.asarray(gid[:, None] == gid[None, :], BF16)
    kk = np.arange(K_BLOCK)
    tri = jnp.asarray(kk[:, None] > kk[None, :], BF16)

    for i in range(depth):
        j = i // 2
        ng = mix_norm_g[i].reshape(1, d)
        if i % 2 == 0:
            qg = jnp.tile(sb_q_norm_g[j], N_HEADS).reshape(1, SB_WIDTH)
            kg = jnp.tile(sb_k_norm_g[j], N_HEADS).reshape(1, SB_WIDTH)
            zg = sg_z_norm_g[j].reshape(1, SG_WIDTH)
            bs = jnp.repeat(sg_b_spatial[j].T, SG_GROUP, axis=1)
            q, kt, v, g = _mixer_in(x2, ng, sb_w_in[j].astype(BF16), qg, kg, zg, bd, sg_w_spatial[j], bs)
            a = _attention(q, kt, v, tri, batch, seq)
            w_out = hyb_w_out[j].astype(BF16)
            x2 = _mixer_out(x2, a, g, w_out[:SB_WIDTH], w_out[SB_WIDTH:])
        else:
            cw = cv_w_dw.shape[2]
            x2 = _conformer(
                x2, ng, cv_w_pw1[j].astype(BF16), cv_b_pw1[j].reshape(1, 2 * cw),
                _pad_rows(_chunk_cols(cv_w_dw[j], CV_CHUNK), 32),
                _chunk_cols(cv_b_dw[j].reshape(1, cw), CV_CHUNK),
                _chunk_cols(cv_ln_g[j].reshape(1, cw), CV_CHUNK),
                _chunk_cols(cv_ln_b[j].reshape(1, cw), CV_CHUNK),
                cv_w_pw2[j].astype(BF16).reshape(cw // CV_CHUNK, CV_CHUNK, d),
                cv_b_pw2[j].reshape(1, d), seq)
        w_up = ffn_w_up[i].astype(BF16)
        x2 = _ffn(
            x2, ffn_norm_g[i].reshape(1, d),
            _chunk_cols(w_up[:, :d_ff], FF_CHUNK), _chunk_cols(w_up[:, d_ff:], FF_CHUNK),
            _pad_rows(_chunk_cols(ffn_w_dw[i], FF_CHUNK), 8),
            _chunk_cols(ffn_b_dw[i].reshape(1, d_ff), FF_CHUNK),
            ffn_w_down[i].astype(BF16).reshape(d_ff // FF_CHUNK, FF_CHUNK, d), seq)
    return x2.reshape(batch, seq, d)
```

```python
import functools

import jax
import jax.numpy as jnp
import numpy as np
from jax import lax
from jax.experimental import pallas as pl
from jax.experimental.pallas import tpu as pltpu

F32 = jnp.float32
BF16 = jnp.bfloat16
EPS = 1e-6

HEAD_DIM = 64
N_HEADS = 8
SB_WIDTH = HEAD_DIM * N_HEADS
SG_GROUP = 64
SG_GROUPS = 8
SG_WIDTH = SG_GROUP * SG_GROUPS
CHUNK = 128
CONV_K = 31
FFN_K = 3

LANES = 128
TOK_TILE = 512
Q_BLOCK = 256
K_BLOCK = 256
FF_CHUNK = 256
CV_CHUNK = 256
CV_ROWS = 32
CV_HALO = 32
FF_HALO = 8
VMEM_LIMIT = 56 * 1024 * 1024
DEAD_LOG2 = -160.0


def _dot(a, b):
    return jnp.dot(a, b, preferred_element_type=F32)


def _rms_norm(x, g):
    return x * lax.rsqrt(jnp.mean(x * x, axis=-1, keepdims=True) + EPS) * g


def _gelu(t):
    return 0.5 * t * (1.0 + lax.erf(t * np.float32(0.7071067811865476)))


def _split_dot(t, m):
    hi = t.astype(BF16)
    lo = (t - hi.astype(F32)).astype(BF16)
    return _dot(hi, m) + _dot(lo, m)


def _const_spec(shape):
    nd = len(shape)
    return pl.BlockSpec(shape, lambda *_: (0,) * nd, pipeline_mode=pl.Buffered(1))


def _params(*sem):
    return pltpu.CompilerParams(dimension_semantics=sem, vmem_limit_bytes=VMEM_LIMIT)


def _mixer_in_kernel(x_ref, ng_ref, w_ref, qg_ref, kg_ref, zg_ref, bd_ref, ws_ref, bs_ref,
                     q_out, kt_out, v_out, g_out):
    tt = x_ref.shape[0]
    h = _rms_norm(x_ref[...], ng_ref[...]).astype(BF16)
    proj = _dot(h, w_ref[...])
    q = proj[:, 0:SB_WIDTH]
    k = proj[:, SB_WIDTH:2 * SB_WIDTH]
    v = proj[:, 2 * SB_WIDTH:3 * SB_WIDTH]
    u = proj[:, 3 * SB_WIDTH:3 * SB_WIDTH + SG_WIDTH]
    z = proj[:, 3 * SB_WIDTH + SG_WIDTH:]
    bd = bd_ref[...]

    def group_norm(t, g):
        ms = _split_dot(t * t, bd) * np.float32(1.0 / HEAD_DIM)
        return t * lax.rsqrt(ms + EPS) * g

    q_out[...] = (group_norm(q, qg_ref[...]) * np.float32(HEAD_DIM ** -0.5 * np.log2(np.e))).astype(BF16)
    kn = group_norm(k, kg_ref[...])
    for j in range(tt // K_BLOCK):
        kt_out[j] = kn[j * K_BLOCK:(j + 1) * K_BLOCK, :].T.astype(BF16)
    v_out[...] = v.astype(BF16)

    ug = _gelu(u)
    zn = group_norm(_gelu(z), zg_ref[...]).astype(BF16)
    row = lax.broadcasted_iota(jnp.int32, (CHUNK, CHUNK), 0)
    col = lax.broadcasted_iota(jnp.int32, (CHUNK, CHUNK), 1)
    lane = lax.broadcasted_iota(jnp.int32, (CHUNK, LANES), 1)
    wm = [jnp.where(col <= row, ws_ref[g], 0.0).astype(BF16) for g in range(SG_GROUPS)]
    bs = bs_ref[...]
    for c in range(tt // CHUNK):
        rows = slice(c * CHUNK, (c + 1) * CHUNK)
        for p in range(SG_WIDTH // LANES):
            cols = slice(p * LANES, (p + 1) * LANES)
            zp = zn[rows, cols]
            s = jnp.where(lane < SG_GROUP, _dot(wm[2 * p], zp), _dot(wm[2 * p + 1], zp))
            g_out[rows, cols] = (ug[rows, cols] * (s + bs[:, cols])).astype(BF16)


def _mixer_in(x2, ng, w_in, qg, kg, zg, bd, ws, bs):
    n, d = x2.shape
    tt = TOK_TILE
    in_w = w_in.shape[1]
    return pl.pallas_call(
        _mixer_in_kernel,
        grid=(n // tt,),
        in_specs=[
            pl.BlockSpec((tt, d), lambda i: (i, 0)),
            _const_spec((1, d)),
            _const_spec((d, in_w)),
            _const_spec((1, SB_WIDTH)),
            _const_spec((1, SB_WIDTH)),
            _const_spec((1, SG_WIDTH)),
            _const_spec((SB_WIDTH, SB_WIDTH)),
            _const_spec((SG_GROUPS, CHUNK, CHUNK)),
            _const_spec((CHUNK, SG_WIDTH)),
        ],
        out_specs=[
            pl.BlockSpec((tt, SB_WIDTH), lambda i: (i, 0)),
            pl.BlockSpec((tt // K_BLOCK, SB_WIDTH, K_BLOCK), lambda i: (i, 0, 0)),
            pl.BlockSpec((tt, SB_WIDTH), lambda i: (i, 0)),
            pl.BlockSpec((tt, SG_WIDTH), lambda i: (i, 0)),
        ],
        out_shape=[
            jax.ShapeDtypeStruct((n, SB_WIDTH), BF16),
            jax.ShapeDtypeStruct((n // K_BLOCK, SB_WIDTH, K_BLOCK), BF16),
            jax.ShapeDtypeStruct((n, SB_WIDTH), BF16),
            jax.ShapeDtypeStruct((n, SG_WIDTH), BF16),
        ],
        compiler_params=_params("parallel"),
        name="mixer_in",
    )(x2, ng, w_in, qg, kg, zg, bd, ws, bs)


def _attn_kernel(q_ref, kt_ref, v_ref, tri_ref, o_ref, qm_ref, acc_ref, car_ref):
    i = pl.program_id(1)
    lane = lax.broadcasted_iota(jnp.int32, (Q_BLOCK, LANES), 1)
    row = lax.broadcasted_iota(jnp.int32, (Q_BLOCK, K_BLOCK), 0)
    col = lax.broadcasted_iota(jnp.int32, (Q_BLOCK, K_BLOCK), 1)
    causal = col < row
    low = lane < HEAD_DIM

    for h in range(N_HEADS):
        qp = q_ref[:, (h // 2) * LANES:(h // 2 + 1) * LANES]
        keep = low if h % 2 == 0 else jnp.logical_not(low)
        qm_ref[h] = jnp.where(keep, qp, jnp.zeros_like(qp))
    acc_ref[...] = jnp.zeros_like(acc_ref)
    car_ref[...] = jnp.zeros_like(car_ref)

    def block(kb, diag):
        k0 = pl.multiple_of(kb * K_BLOCK, K_BLOCK)
        grp = [slice((h // 2) * LANES, (h // 2 + 1) * LANES) for h in range(N_HEADS)]
        zs = [_dot(qm_ref[h], kt_ref[kb, grp[h], :]) for h in range(N_HEADS)]
        args = []
        for h in range(N_HEADS):
            z = zs[h]
            sp = jnp.log2(1.0 + jnp.exp2(-jnp.abs(z)))
            ls = jnp.minimum(z, 0.0) - sp
            lk = ls - z
            if diag:
                lk = jnp.where(causal, lk, 0.0)
            car = car_ref[h]
            hi = lk.astype(BF16)
            lo = (lk - hi.astype(F32)).astype(BF16)
            tail = _dot(jnp.concatenate([hi, lo], axis=1), tri_ref[...])
            args.append(ls + tail + jnp.concatenate([car, car], axis=1))
            car_ref[h] = car + jnp.sum(lk, axis=1, keepdims=True)
        for h in range(N_HEADS):
            w = jnp.exp2(args[h])
            if diag:
                w = jnp.where(causal, w, 0.0)
            acc_ref[h] += _dot(w.astype(BF16), v_ref[pl.ds(k0, K_BLOCK), grp[h]])

    def max_carry():
        m = car_ref[0]
        for h in range(1, N_HEADS):
            m = jnp.maximum(m, car_ref[h])
        return jnp.max(m)

    block(i, True)

    def cond(state):
        kb, mx = state
        return jnp.logical_and(kb >= 0, mx > DEAD_LOG2)

    def body(state):
        kb, _ = state
        block(kb, False)
        return kb - 1, max_carry()

    lax.while_loop(cond, body, (i - 1, max_carry()))

    for p in range(N_HEADS // 2):
        o_ref[:, p * LANES:(p + 1) * LANES] = jnp.where(
            low, acc_ref[2 * p], acc_ref[2 * p + 1]).astype(o_ref.dtype)


def _attention(q, kt, v, tri, batch, seq):
    nq = seq // Q_BLOCK
    nk = seq // K_BLOCK
    return pl.pallas_call(
        _attn_kernel,
        grid=(batch, nq),
        in_specs=[
            pl.BlockSpec((Q_BLOCK, SB_WIDTH), lambda b, i: (b * nq + i, 0)),
            pl.BlockSpec((nk, SB_WIDTH, K_BLOCK), lambda b, i: (b, 0, 0), pipeline_mode=pl.Buffered(1)),
            pl.BlockSpec((seq, SB_WIDTH), lambda b, i: (b, 0), pipeline_mode=pl.Buffered(1)),
            _const_spec((2 * K_BLOCK, K_BLOCK)),
        ],
        out_specs=pl.BlockSpec((Q_BLOCK, SB_WIDTH), lambda b, i: (b * nq + i, 0)),
        out_shape=jax.ShapeDtypeStruct((batch * seq, SB_WIDTH), BF16),
        scratch_shapes=[
            pltpu.VMEM((N_HEADS, Q_BLOCK, LANES), BF16),
            pltpu.VMEM((N_HEADS, Q_BLOCK, LANES), F32),
            pltpu.VMEM((N_HEADS, Q_BLOCK, LANES), F32),
        ],
        compiler_params=_params("parallel", "parallel"),
        name="sb_attention",
    )(q, kt, v, tri)


def _mixer_out_kernel(x_ref, a_ref, g_ref, wa_ref, wg_ref, o_ref):
    o_ref[...] = x_ref[...] + _dot(a_ref[...], wa_ref[...]) + _dot(g_ref[...], wg_ref[...])


def _mixer_out(x2, a, g, wa, wg):
    n, d = x2.shape
    tt = TOK_TILE
    return pl.pallas_call(
        _mixer_out_kernel,
        grid=(n // tt,),
        in_specs=[
            pl.BlockSpec((tt, d), lambda i: (i, 0)),
            pl.BlockSpec((tt, SB_WIDTH), lambda i: (i, 0)),
            pl.BlockSpec((tt, SG_WIDTH), lambda i: (i, 0)),
            _const_spec((SB_WIDTH, d)),
            _const_spec((SG_WIDTH, d)),
        ],
        out_specs=pl.BlockSpec((tt, d), lambda i: (i, 0)),
        out_shape=jax.ShapeDtypeStruct((n, d), F32),
        compiler_params=_params("parallel"),
        name="mixer_out",
    )(x2, a, g, wa, wg)


def _ffn_kernel(tiles_per_seq, x_ref, ng_ref, wg_ref, wv_ref, wdw_ref, bdw_ref, wd_ref, o_ref, cbuf_ref):
    t = pl.program_id(0)
    tt = x_ref.shape[0]
    n_chunks = wg_ref.shape[0]

    @pl.when(t == 0)
    def _():
        cbuf_ref[...] = jnp.zeros_like(cbuf_ref)

    first = (t % tiles_per_seq) == 0
    x = x_ref[...]
    h = _rms_norm(x, ng_ref[...]).astype(BF16)
    acc = x
    for c in range(n_chunks):
        gate = _dot(h, wg_ref[c])
        val = _dot(h, wv_ref[c])
        prev = jnp.where(first, 0.0, cbuf_ref[c, tt:tt + FF_HALO, :])
        cbuf_ref[c, 0:FF_HALO, :] = prev
        cbuf_ref[c, FF_HALO:FF_HALO + tt, :] = gate
        wk = wdw_ref[c]
        conv = bdw_ref[c] + gate * wk[FFN_K - 1:FFN_K, :]
        for k in range(FFN_K - 1):
            off = FF_HALO - (FFN_K - 1) + k
            conv = conv + cbuf_ref[c, off:off + tt, :] * wk[k:k + 1, :]
        act = conv * jax.nn.sigmoid(conv) * val
        acc = acc + _dot(act.astype(BF16), wd_ref[c])
    o_ref[...] = acc


def _ffn(x2, ng, wg, wv, wdw, bdw, wd, seq):
    n, d = x2.shape
    tt = TOK_TILE
    nc = wg.shape[0]
    return pl.pallas_call(
        functools.partial(_ffn_kernel, seq // tt),
        grid=(n // tt,),
        in_specs=[
            pl.BlockSpec((tt, d), lambda i: (i, 0)),
            _const_spec((1, d)),
            _const_spec((nc, d, FF_CHUNK)),
            _const_spec((nc, d, FF_CHUNK)),
            _const_spec((nc, 8, FF_CHUNK)),
            _const_spec((nc, 1, FF_CHUNK)),
            _const_spec((nc, FF_CHUNK, d)),
        ],
        out_specs=pl.BlockSpec((tt, d), lambda i: (i, 0)),
        out_shape=jax.ShapeDtypeStruct((n, d), F32),
        scratch_shapes=[pltpu.VMEM((nc, tt + FF_HALO, FF_CHUNK), F32)],
        compiler_params=_params("arbitrary"),
        name="conv_glu_ffn",
    )(x2, ng, wg, wv, wdw, bdw, wd)


def _conformer_kernel(tiles_per_seq, x_ref, ng_ref, w1_ref, b1_ref, wdw_ref, bdw_ref, lng_ref, lnb_ref,
                      w2_ref, b2_ref, o_ref, ybuf_ref, cv_ref):
    t = pl.program_id(0)
    tt = x_ref.shape[0]
    cw = w1_ref.shape[1] // 2
    n_chunks = cw // CV_CHUNK

    @pl.when(t == 0)
    def _():
        ybuf_ref[...] = jnp.zeros_like(ybuf_ref)

    first = (t % tiles_per_seq) == 0
    x = x_ref[...]
    h = _rms_norm(x, ng_ref[...]).astype(BF16)
    pre = _dot(h, w1_ref[...]) + b1_ref[...]
    y = pre[:, :cw] * jax.nn.sigmoid(pre[:, cw:])
    for c in range(n_chunks):
        prev = jnp.where(first, 0.0, ybuf_ref[c, tt:tt + CV_HALO, :])
        ybuf_ref[c, CV_HALO:CV_HALO + tt, :] = y[:, c * CV_CHUNK:(c + 1) * CV_CHUNK]
        ybuf_ref[c, 0:CV_HALO, :] = prev

    def conv_chunk(c, carry):
        bias = bdw_ref[c]
        for rb in range(tt // CV_ROWS):
            r0 = rb * CV_ROWS
            acc = jnp.broadcast_to(bias, (CV_ROWS, CV_CHUNK))
            for k in range(CONV_K):
                off = CV_HALO - (CONV_K - 1) + k + r0
                acc = acc + ybuf_ref[c, off:off + CV_ROWS, :] * wdw_ref[c, k:k + 1, :]
            cv_ref[c, r0:r0 + CV_ROWS, :] = acc
        return carry

    lax.fori_loop(0, n_chunks, conv_chunk, 0)

    s1 = cv_ref[0].sum(axis=-1, keepdims=True)
    for c in range(1, n_chunks):
        s1 = s1 + cv_ref[c].sum(axis=-1, keepdims=True)
    mu = s1 * np.float32(1.0 / cw)
    s2 = jnp.zeros_like(mu)
    for c in range(n_chunks):
        dc = cv_ref[c] - mu
        s2 = s2 + (dc * dc).sum(axis=-1, keepdims=True)
    rstd = lax.rsqrt(s2 * np.float32(1.0 / cw) + EPS)
    out = x + b2_ref[...]
    for c in range(n_chunks):
        ln = (cv_ref[c] - mu) * rstd * lng_ref[c] + lnb_ref[c]
        a = ln * jax.nn.sigmoid(ln)
        out = out + _dot(a.astype(BF16), w2_ref[c])
    o_ref[...] = out


def _conformer(x2, ng, w1, b1, wdw, bdw, lng, lnb, w2, b2, seq):
    n, d = x2.shape
    tt = TOK_TILE
    cw = w1.shape[1] // 2
    nc = cw // CV_CHUNK
    return pl.pallas_call(
        functools.partial(_conformer_kernel, seq // tt),
        grid=(n // tt,),
        in_specs=[
            pl.BlockSpec((tt, d), lambda i: (i, 0)),
            _const_spec((1, d)),
            _const_spec((d, 2 * cw)),
            _const_spec((1, 2 * cw)),
            _const_spec((nc, 32, CV_CHUNK)),
            _const_spec((nc, 1, CV_CHUNK)),
            _const_spec((nc, 1, CV_CHUNK)),
            _const_spec((nc, 1, CV_CHUNK)),
            _const_spec((nc, CV_CHUNK, d)),
            _const_spec((1, d)),
        ],
        out_specs=pl.BlockSpec((tt, d), lambda i: (i, 0)),
        out_shape=jax.ShapeDtypeStruct((n, d), F32),
        scratch_shapes=[
            pltpu.VMEM((nc, tt + CV_HALO, CV_CHUNK), F32),
            pltpu.VMEM((nc, tt, CV_CHUNK), F32),
        ],
        compiler_params=_params("arbitrary"),
        name="conformer",
    )(x2, ng, w1, b1, wdw, bdw, lng, lnb, w2, b2)


def _chunk_cols(w, width):
    lead = w.shape[:-1]
    n = w.shape[-1] // width
    return jnp.moveaxis(w.reshape(lead + (n, width)), -2, 0)


def _pad_rows(w, rows):
    return jnp.pad(w, ((0, 0), (0, rows - w.shape[1]), (0, 0)))


def kernel(x, mix_norm_g, sb_w_in, sb_q_norm_g, sb_k_norm_g, sg_z_norm_g, sg_w_spatial, sg_b_spatial,
           hyb_w_out, cv_w_pw1, cv_b_pw1, cv_w_dw, cv_b_dw, cv_ln_g, cv_ln_b, cv_w_pw2, cv_b_pw2,
           ffn_norm_g, ffn_w_up, ffn_w_dw, ffn_b_dw, ffn_w_down):
    batch, seq, d = x.shape
    depth = mix_norm_g.shape[0]
    d_ff = ffn_w_down.shape[1]
    x2 = x.reshape(batch * seq, d)

    gid = np.arange(SB_WIDTH) // HEAD_DIM
    bd = jnp.asarray(gid[:, None] == gid[None, :], BF16)
    kk = np.arange(K_BLOCK)
    tri = kk[:, None] > kk[None, :]
    tri = jnp.asarray(np.concatenate([tri, tri], axis=0), BF16)

    for i in range(depth):
        j = i // 2
        ng = mix_norm_g[i].reshape(1, d)
        if i % 2 == 0:
            qg = jnp.tile(sb_q_norm_g[j], N_HEADS).reshape(1, SB_WIDTH)
            kg = jnp.tile(sb_k_norm_g[j], N_HEADS).reshape(1, SB_WIDTH)
            zg = sg_z_norm_g[j].reshape(1, SG_WIDTH)
            bs = jnp.repeat(sg_b_spatial[j].T, SG_GROUP, axis=1)
            q, kt, v, g = _mixer_in(x2, ng, sb_w_in[j].astype(BF16), qg, kg, zg, bd, sg_w_spatial[j], bs)
            a = _attention(q, kt, v, tri, batch, seq)
            w_out = hyb_w_out[j].astype(BF16)
            x2 = _mixer_out(x2, a, g, w_out[:SB_WIDTH], w_out[SB_WIDTH:])
        else:
            cw = cv_w_dw.shape[2]
            x2 = _conformer(
                x2, ng, cv_w_pw1[j].astype(BF16), cv_b_pw1[j].reshape(1, 2 * cw),
                _pad_rows(_chunk_cols(cv_w_dw[j], CV_CHUNK), 32),
                _chunk_cols(cv_b_dw[j].reshape(1, cw), CV_CHUNK),
                _chunk_cols(cv_ln_g[j].reshape(1, cw), CV_CHUNK),
                _chunk_cols(cv_ln_b[j].reshape(1, cw), CV_CHUNK),
                cv_w_pw2[j].astype(BF16).reshape(cw // CV_CHUNK, CV_CHUNK, d),
                cv_b_pw2[j].reshape(1, d), seq)
        w_up = ffn_w_up[i].astype(BF16)
        x2 = _ffn(
            x2, ffn_norm_g[i].reshape(1, d),
            _chunk_cols(w_up[:, :d_ff], FF_CHUNK), _chunk_cols(w_up[:, d_ff:], FF_CHUNK),
            _pad_rows(_chunk_cols(ffn_w_dw[i], FF_CHUNK), 8),
            _chunk_cols(ffn_b_dw[i].reshape(1, d_ff), FF_CHUNK),
            ffn_w_down[i].astype(BF16).reshape(d_ff // FF_CHUNK, FF_CHUNK, d), seq)
    return x2.reshape(batch, seq, d)
```

```python
import functools

import jax
import jax.numpy as jnp
import numpy as np
from jax import lax
from jax.experimental import pallas as pl
from jax.experimental.pallas import tpu as pltpu

F32 = jnp.float32
BF16 = jnp.bfloat16
EPS = 1e-6

HEAD_DIM = 64
N_HEADS = 8
SB_WIDTH = HEAD_DIM * N_HEADS
SG_GROUP = 64
SG_GROUPS = 8
SG_WIDTH = SG_GROUP * SG_GROUPS
CHUNK = 128
CONV_K = 31
FFN_K = 3

LANES = 128
TOK_TILE = 512
Q_BLOCK = 256
MXU_TILE = 256
K_BLOCK = MXU_TILE
FF_CHUNK = 256
CV_CHUNK = 128
CV_ROWS = 32
CV_HALO = 32
FF_HALO = 8
VMEM_LIMIT = 56 * 1024 * 1024
DEAD_LOG2 = -160.0


def _dot(a, b):
    return jnp.dot(a, b, preferred_element_type=F32)


def _rms_norm(x, g):
    return x * lax.rsqrt(jnp.mean(x * x, axis=-1, keepdims=True) + EPS) * g


def _gelu(t):
    return 0.5 * t * (1.0 + lax.erf(t * np.float32(0.7071067811865476)))


def _split_dot(t, m):
    hi = t.astype(BF16)
    lo = (t - hi.astype(F32)).astype(BF16)
    return _dot(hi, m) + _dot(lo, m)


def _const_spec(shape):
    nd = len(shape)
    return pl.BlockSpec(shape, lambda *_: (0,) * nd, pipeline_mode=pl.Buffered(1))


def _params(*sem):
    return pltpu.CompilerParams(dimension_semantics=sem, vmem_limit_bytes=VMEM_LIMIT)


def _mixer_in_kernel(x_ref, ng_ref, w_ref, qg_ref, kg_ref, zg_ref, bd_ref, ws_ref, bs_ref,
                     q_out, kt_out, v_out, g_out):
    tt = x_ref.shape[0]
    h = _rms_norm(x_ref[...], ng_ref[...]).astype(BF16)
    proj = _dot(h, w_ref[...])
    q = proj[:, 0:SB_WIDTH]
    k = proj[:, SB_WIDTH:2 * SB_WIDTH]
    v = proj[:, 2 * SB_WIDTH:3 * SB_WIDTH]
    u = proj[:, 3 * SB_WIDTH:3 * SB_WIDTH + SG_WIDTH]
    z = proj[:, 3 * SB_WIDTH + SG_WIDTH:]
    bd = bd_ref[...]

    def group_norm(t, g):
        t2 = (t * t).astype(BF16)
        ms = jnp.concatenate(
            [_dot(t2[:, j:j + MXU_TILE], bd) for j in range(0, t.shape[1], MXU_TILE)], axis=1)
        return t * lax.rsqrt(ms * np.float32(1.0 / HEAD_DIM) + EPS) * g

    q_out[...] = (group_norm(q, qg_ref[...]) * np.float32(HEAD_DIM ** -0.5 * np.log2(np.e))).astype(BF16)
    kn = group_norm(k, kg_ref[...])
    for j in range(tt // K_BLOCK):
        kt_out[j] = kn[j * K_BLOCK:(j + 1) * K_BLOCK, :].T.astype(BF16)
    v_out[...] = v.astype(BF16)

    ug = _gelu(u)
    zn = group_norm(_gelu(z), zg_ref[...]).astype(BF16)
    row = lax.broadcasted_iota(jnp.int32, (CHUNK, CHUNK), 0)
    col = lax.broadcasted_iota(jnp.int32, (CHUNK, CHUNK), 1)
    n_ch = tt // CHUNK
    lane = lax.broadcasted_iota(jnp.int32, (CHUNK, n_ch * LANES), 1)
    low = (lane % LANES) < SG_GROUP
    wm = [jnp.where(col <= row, ws_ref[g], 0.0).astype(BF16) for g in range(SG_GROUPS)]
    bs = bs_ref[...]
    for p in range(SG_WIDTH // LANES):
        cols = slice(p * LANES, (p + 1) * LANES)
        zp = jnp.concatenate([zn[c * CHUNK:(c + 1) * CHUNK, cols] for c in range(n_ch)], axis=1)
        rhs = jnp.concatenate(
            [jnp.where(low, zp, jnp.zeros_like(zp)), jnp.where(low, jnp.zeros_like(zp), zp)], axis=0)
        s = _dot(jnp.concatenate([wm[2 * p], wm[2 * p + 1]], axis=1), rhs)
        for c in range(n_ch):
            rows = slice(c * CHUNK, (c + 1) * CHUNK)
            g_out[rows, cols] = (ug[rows, cols] * (s[:, c * LANES:(c + 1) * LANES] + bs[:, cols])).astype(BF16)


def _mixer_in(x2, ng, w_in, qg, kg, zg, bd, ws, bs):
    n, d = x2.shape
    tt = TOK_TILE
    in_w = w_in.shape[1]
    return pl.pallas_call(
        _mixer_in_kernel,
        grid=(n // tt,),
        in_specs=[
            pl.BlockSpec((tt, d), lambda i: (i, 0)),
            _const_spec((1, d)),
            _const_spec((d, in_w)),
            _const_spec((1, SB_WIDTH)),
            _const_spec((1, SB_WIDTH)),
            _const_spec((1, SG_WIDTH)),
            _const_spec((MXU_TILE, MXU_TILE)),
            _const_spec((SG_GROUPS, CHUNK, CHUNK)),
            _const_spec((CHUNK, SG_WIDTH)),
        ],
        out_specs=[
            pl.BlockSpec((tt, SB_WIDTH), lambda i: (i, 0)),
            pl.BlockSpec((tt // K_BLOCK, SB_WIDTH, K_BLOCK), lambda i: (i, 0, 0)),
            pl.BlockSpec((tt, SB_WIDTH), lambda i: (i, 0)),
            pl.BlockSpec((tt, SG_WIDTH), lambda i: (i, 0)),
        ],
        out_shape=[
            jax.ShapeDtypeStruct((n, SB_WIDTH), BF16),
            jax.ShapeDtypeStruct((n // K_BLOCK, SB_WIDTH, K_BLOCK), BF16),
            jax.ShapeDtypeStruct((n, SB_WIDTH), BF16),
            jax.ShapeDtypeStruct((n, SG_WIDTH), BF16),
        ],
        compiler_params=_params("parallel"),
        name="mixer_in",
    )(x2, ng, w_in, qg, kg, zg, bd, ws, bs)


def _attn_kernel(q_ref, kt_ref, v_ref, tri_ref, o_ref, qm_ref, acc_ref, car_ref):
    i = pl.program_id(1)
    lane = lax.broadcasted_iota(jnp.int32, (Q_BLOCK, LANES), 1)
    row = lax.broadcasted_iota(jnp.int32, (Q_BLOCK, K_BLOCK), 0)
    col = lax.broadcasted_iota(jnp.int32, (Q_BLOCK, K_BLOCK), 1)
    causal = col < row
    low = lane < HEAD_DIM

    for h in range(N_HEADS):
        qp = q_ref[:, (h // 2) * LANES:(h // 2 + 1) * LANES]
        keep = low if h % 2 == 0 else jnp.logical_not(low)
        qm_ref[h] = jnp.where(keep, qp, jnp.zeros_like(qp))
    acc_ref[...] = jnp.zeros_like(acc_ref)
    car_ref[...] = jnp.zeros_like(car_ref)

    def block(kb, diag):
        k0 = pl.multiple_of(kb * K_BLOCK, K_BLOCK)
        grp = [slice((h // 2) * LANES, (h // 2 + 1) * LANES) for h in range(N_HEADS)]
        zs = [_dot(qm_ref[h], kt_ref[kb, grp[h], :]) for h in range(N_HEADS)]
        args = []
        for h in range(N_HEADS):
            z = zs[h]
            neg_abs = lax.bitcast_convert_type(
                lax.bitcast_convert_type(z, jnp.uint32) | jnp.uint32(0x80000000), F32)
            sp = jnp.log2(1.0 + jnp.exp2(neg_abs))
            ls = jnp.minimum(z, 0.0) - sp
            lk = ls - z
            if diag:
                lk = jnp.where(causal, lk, 0.0)
            car = car_ref[h]
            tail = _dot(lk.astype(BF16), tri_ref[...])
            args.append(ls + tail + jnp.concatenate([car, car], axis=1))
            car_ref[h] = car + jnp.sum(lk, axis=1, keepdims=True)
        for h in range(N_HEADS):
            w = jnp.exp2(args[h])
            if diag:
                w = jnp.where(causal, w, 0.0)
            acc_ref[h] += _dot(w.astype(BF16), v_ref[pl.ds(k0, K_BLOCK), grp[h]])

    def max_carry():
        m = car_ref[0]
        for h in range(1, N_HEADS):
            m = jnp.maximum(m, car_ref[h])
        return jnp.max(m)

    block(i, True)

    def cond(state):
        kb, mx = state
        return jnp.logical_and(kb >= 0, mx > DEAD_LOG2)

    def body(state):
        kb, _ = state
        block(kb, False)
        return kb - 1, max_carry()

    lax.while_loop(cond, body, (i - 1, max_carry()))

    for p in range(N_HEADS // 2):
        o_ref[:, p * LANES:(p + 1) * LANES] = jnp.where(
            low, acc_ref[2 * p], acc_ref[2 * p + 1]).astype(o_ref.dtype)


def _attention(q, kt, v, tri, batch, seq):
    nq = seq // Q_BLOCK
    nk = seq // K_BLOCK
    return pl.pallas_call(
        _attn_kernel,
        grid=(batch, nq),
        in_specs=[
            pl.BlockSpec((Q_BLOCK, SB_WIDTH), lambda b, i: (b * nq + i, 0)),
            pl.BlockSpec((nk, SB_WIDTH, K_BLOCK), lambda b, i: (b, 0, 0), pipeline_mode=pl.Buffered(1)),
            pl.BlockSpec((seq, SB_WIDTH), lambda b, i: (b, 0), pipeline_mode=pl.Buffered(1)),
            _const_spec((K_BLOCK, K_BLOCK)),
        ],
        out_specs=pl.BlockSpec((Q_BLOCK, SB_WIDTH), lambda b, i: (b * nq + i, 0)),
        out_shape=jax.ShapeDtypeStruct((batch * seq, SB_WIDTH), BF16),
        scratch_shapes=[
            pltpu.VMEM((N_HEADS, Q_BLOCK, LANES), BF16),
            pltpu.VMEM((N_HEADS, Q_BLOCK, LANES), F32),
            pltpu.VMEM((N_HEADS, Q_BLOCK, LANES), F32),
        ],
        compiler_params=_params("parallel", "parallel"),
        name="sb_attention",
    )(q, kt, v, tri)


def _ffn_kernel(tiles_per_seq, has_mix, *refs):
    if has_mix:
        x_ref, a_ref, g_ref, wa_ref, wo_ref = refs[:5]
        refs = (x_ref,) + refs[5:]
    x_ref, ng_ref, wg_ref, wv_ref, wdw_ref, bdw_ref, wd_ref, o_ref, cbuf_ref, act_ref = refs
    t = pl.program_id(0)
    tt = x_ref.shape[0]
    n_chunks = wg_ref.shape[0]

    @pl.when(t == 0)
    def _():
        cbuf_ref[...] = jnp.zeros_like(cbuf_ref)

    first = (t % tiles_per_seq) == 0
    x = x_ref[...]
    if has_mix:
        x = x + _dot(a_ref[...], wa_ref[...]) + _dot(g_ref[...], wo_ref[...])
    h = _rms_norm(x, ng_ref[...]).astype(BF16)
    for c in range(n_chunks):
        gate = _dot(h, wg_ref[c])
        val = _dot(h, wv_ref[c])
        wk = wdw_ref[c]
        bias = bdw_ref[c]
        for j in range(FF_CHUNK // LANES):
            b = c * (FF_CHUNK // LANES) + j
            cols = slice(j * LANES, (j + 1) * LANES)
            gj = gate[:, cols]
            prev = jnp.where(first, 0.0, cbuf_ref[b, tt:tt + FF_HALO, :])
            cbuf_ref[b, 0:FF_HALO, :] = prev
            cbuf_ref[b, FF_HALO:FF_HALO + tt, :] = gj
            conv = bias[:, cols] + gj * wk[FFN_K - 1:FFN_K, cols]
            for k in range(FFN_K - 1):
                off = FF_HALO - (FFN_K - 1) + k
                conv = conv + cbuf_ref[b, off:off + tt, :] * wk[k:k + 1, cols]
            act = conv * jax.nn.sigmoid(conv) * val[:, cols]
            act_ref[:, c * FF_CHUNK + j * LANES:c * FF_CHUNK + (j + 1) * LANES] = act.astype(BF16)
    o_ref[...] = x + _dot(act_ref[...], wd_ref[...])


def _ffn(x2, mix, ng, wg, wv, wdw, bdw, wd, seq):
    n, d = x2.shape
    tt = TOK_TILE
    nc = wg.shape[0]

    def tok(width):
        return pl.BlockSpec((tt, width), lambda i: (i, 0))

    mix_specs = [] if mix is None else [
        tok(SB_WIDTH), tok(SG_WIDTH), _const_spec((SB_WIDTH, d)), _const_spec((SG_WIDTH, d))]
    return pl.pallas_call(
        functools.partial(_ffn_kernel, seq // tt, mix is not None),
        grid=(n // tt,),
        in_specs=[tok(d)] + mix_specs + [
            _const_spec((1, d)),
            _const_spec((nc, d, FF_CHUNK)),
            _const_spec((nc, d, FF_CHUNK)),
            _const_spec((nc, 8, FF_CHUNK)),
            _const_spec((nc, 1, FF_CHUNK)),
            _const_spec((nc * FF_CHUNK, d)),
        ],
        out_specs=pl.BlockSpec((tt, d), lambda i: (i, 0)),
        out_shape=jax.ShapeDtypeStruct((n, d), F32),
        scratch_shapes=[
            pltpu.VMEM((nc * FF_CHUNK // LANES, tt + FF_HALO, LANES), F32),
            pltpu.VMEM((tt, nc * FF_CHUNK), BF16),
        ],
        compiler_params=_params("arbitrary"),
        name="conv_glu_ffn",
    )(x2, *(() if mix is None else mix), ng, wg, wv, wdw, bdw, wd)


def _conformer_kernel(tiles_per_seq, x_ref, ng_ref, w1_ref, b1_ref, wdw_ref, bdw_ref, lng_ref, lnb_ref,
                      w2_ref, b2_ref, o_ref, ybuf_ref, cv_ref, act_ref):
    t = pl.program_id(0)
    tt = x_ref.shape[0]
    cw = w1_ref.shape[1] // 2
    n_chunks = cw // CV_CHUNK

    @pl.when(t == 0)
    def _():
        ybuf_ref[...] = jnp.zeros_like(ybuf_ref)

    first = (t % tiles_per_seq) == 0
    x = x_ref[...]
    h = _rms_norm(x, ng_ref[...]).astype(BF16)
    pre = _dot(h, w1_ref[...]) + b1_ref[...]
    y = pre[:, :cw] * jax.nn.sigmoid(pre[:, cw:])
    for c in range(n_chunks):
        prev = jnp.where(first, 0.0, ybuf_ref[c, tt:tt + CV_HALO, :])
        ybuf_ref[c, CV_HALO:CV_HALO + tt, :] = y[:, c * CV_CHUNK:(c + 1) * CV_CHUNK]
        ybuf_ref[c, 0:CV_HALO, :] = prev

    def conv_chunk(c, carry):
        bias = bdw_ref[c]
        for rb in range(tt // CV_ROWS):
            r0 = rb * CV_ROWS
            acc = jnp.broadcast_to(bias, (CV_ROWS, CV_CHUNK))
            for k in range(CONV_K):
                off = CV_HALO - (CONV_K - 1) + k + r0
                acc = acc + ybuf_ref[c, off:off + CV_ROWS, :] * wdw_ref[c, k:k + 1, :]
            cv_ref[c, r0:r0 + CV_ROWS, :] = acc
        return carry

    lax.fori_loop(0, n_chunks, conv_chunk, 0)

    tot = cv_ref[0]
    for c in range(1, n_chunks):
        tot = tot + cv_ref[c]
    mu = tot.sum(axis=-1, keepdims=True) * np.float32(1.0 / cw)
    sq = jnp.zeros_like(tot)
    for c in range(n_chunks):
        dc = cv_ref[c] - mu
        sq = sq + dc * dc
    rstd = lax.rsqrt(sq.sum(axis=-1, keepdims=True) * np.float32(1.0 / cw) + EPS)
    for c in range(n_chunks):
        ln = (cv_ref[c] - mu) * rstd * lng_ref[c] + lnb_ref[c]
        act_ref[:, c * CV_CHUNK:(c + 1) * CV_CHUNK] = (ln * jax.nn.sigmoid(ln)).astype(BF16)
    o_ref[...] = x + b2_ref[...] + _dot(act_ref[...], w2_ref[...])


def _conformer(x2, ng, w1, b1, wdw, bdw, lng, lnb, w2, b2, seq):
    n, d = x2.shape
    tt = TOK_TILE
    cw = w1.shape[1] // 2
    nc = cw // CV_CHUNK
    return pl.pallas_call(
        functools.partial(_conformer_kernel, seq // tt),
        grid=(n // tt,),
        in_specs=[
            pl.BlockSpec((tt, d), lambda i: (i, 0)),
            _const_spec((1, d)),
            _const_spec((d, 2 * cw)),
            _const_spec((1, 2 * cw)),
            _const_spec((nc, 32, CV_CHUNK)),
            _const_spec((nc, 1, CV_CHUNK)),
            _const_spec((nc, 1, CV_CHUNK)),
            _const_spec((nc, 1, CV_CHUNK)),
            _const_spec((cw, d)),
            _const_spec((1, d)),
        ],
        out_specs=pl.BlockSpec((tt, d), lambda i: (i, 0)),
        out_shape=jax.ShapeDtypeStruct((n, d), F32),
        scratch_shapes=[
            pltpu.VMEM((nc, tt + CV_HALO, CV_CHUNK), F32),
            pltpu.VMEM((nc, tt, CV_CHUNK), F32),
            pltpu.VMEM((tt, cw), BF16),
        ],
        compiler_params=_params("arbitrary"),
        name="conformer",
    )(x2, ng, w1, b1, wdw, bdw, lng, lnb, w2, b2)


def _chunk_cols(w, width):
    lead = w.shape[:-1]
    n = w.shape[-1] // width
    return jnp.moveaxis(w.reshape(lead + (n, width)), -2, 0)


def _pad_rows(w, rows):
    return jnp.pad(w, ((0, 0), (0, rows - w.shape[1]), (0, 0)))


def kernel(x, mix_norm_g, sb_w_in, sb_q_norm_g, sb_k_norm_g, sg_z_norm_g, sg_w_spatial, sg_b_spatial,
           hyb_w_out, cv_w_pw1, cv_b_pw1, cv_w_dw, cv_b_dw, cv_ln_g, cv_ln_b, cv_w_pw2, cv_b_pw2,
           ffn_norm_g, ffn_w_up, ffn_w_dw, ffn_b_dw, ffn_w_down):
    batch, seq, d = x.shape
    depth = mix_norm_g.shape[0]
    d_ff = ffn_w_down.shape[1]
    x2 = x.reshape(batch * seq, d)

    gid = np.arange(MXU_TILE) // HEAD_DIM
    bd = jnp.asarray(gid[:, None] == gid[None, :], BF16)
    kk = np.arange(K_BLOCK)
    tri = jnp.asarray(kk[:, None] > kk[None, :], BF16)

    for i in range(depth):
        j = i // 2
        ng = mix_norm_g[i].reshape(1, d)
        if i % 2 == 0:
            qg = jnp.tile(sb_q_norm_g[j], N_HEADS).reshape(1, SB_WIDTH)
            kg = jnp.tile(sb_k_norm_g[j], N_HEADS).reshape(1, SB_WIDTH)
            zg = sg_z_norm_g[j].reshape(1, SG_WIDTH)
            bs = jnp.repeat(sg_b_spatial[j].T, SG_GROUP, axis=1)
            q, kt, v, g = _mixer_in(x2, ng, sb_w_in[j].astype(BF16), qg, kg, zg, bd, sg_w_spatial[j], bs)
            a = _attention(q, kt, v, tri, batch, seq)
            w_out = hyb_w_out[j].astype(BF16)
            mix = (a, g, w_out[:SB_WIDTH], w_out[SB_WIDTH:])
        else:
            cw = cv_w_dw.shape[2]
            x2 = _conformer(
                x2, ng, cv_w_pw1[j].astype(BF16), cv_b_pw1[j].reshape(1, 2 * cw),
                _pad_rows(_chunk_cols(cv_w_dw[j], CV_CHUNK), 32),
                _chunk_cols(cv_b_dw[j].reshape(1, cw), CV_CHUNK),
                _chunk_cols(cv_ln_g[j].reshape(1, cw), CV_CHUNK),
                _chunk_cols(cv_ln_b[j].reshape(1, cw), CV_CHUNK),
                cv_w_pw2[j].astype(BF16),
                cv_b_pw2[j].reshape(1, d), seq)
            mix = None
        w_up = ffn_w_up[i].astype(BF16)
        x2 = _ffn(
            x2, mix, ffn_norm_g[i].reshape(1, d),
            _chunk_cols(w_up[:, :d_ff], FF_CHUNK), _chunk_cols(w_up[:, d_ff:], FF_CHUNK),
            _pad_rows(_chunk_cols(ffn_w_dw[i], FF_CHUNK), 8),
            _chunk_cols(ffn_b_dw[i].reshape(1, d_ff), FF_CHUNK),
            ffn_w_down[i].astype(BF16), seq)
    return x2.reshape(batch, seq, d)
```

```python
import functools

import jax
import jax.numpy as jnp
import numpy as np
from jax import lax
from jax.experimental import pallas as pl
from jax.experimental.pallas import tpu as pltpu

F32 = jnp.float32
BF16 = jnp.bfloat16
EPS = 1e-6

HEAD_DIM = 64
N_HEADS = 8
SB_WIDTH = HEAD_DIM * N_HEADS
SG_GROUP = 64
SG_GROUPS = 8
SG_WIDTH = SG_GROUP * SG_GROUPS
CHUNK = 128
CONV_K = 31
FFN_K = 3

LANES = 128
TOK_TILE = 512
BIG_TILE = 1024
Q_BLOCK = 256
Q_PER_STEP = 2
MXU_TILE = 256
K_BLOCK = MXU_TILE
FF_CHUNK = 256
CV_CHUNK = 128
CV_ROWS = 32
CV_HALO = 32
FF_HALO = 8
VMEM_LIMIT = 56 * 1024 * 1024
DEAD_LOG2 = -160.0


def _dot(a, b):
    return jnp.dot(a, b, preferred_element_type=F32)


def _rms_norm(x, g):
    return x * lax.rsqrt(jnp.mean(x * x, axis=-1, keepdims=True) + EPS) * g


def _gelu(t):
    return 0.5 * t * (1.0 + lax.erf(t * np.float32(0.7071067811865476)))


def _split_dot(t, m):
    hi = t.astype(BF16)
    lo = (t - hi.astype(F32)).astype(BF16)
    return _dot(hi, m) + _dot(lo, m)


def _const_spec(shape):
    nd = len(shape)
    return pl.BlockSpec(shape, lambda *_: (0,) * nd, pipeline_mode=pl.Buffered(1))


def _params(*sem):
    return pltpu.CompilerParams(dimension_semantics=sem, vmem_limit_bytes=VMEM_LIMIT)


def _mixer_in_kernel(x_ref, ng_ref, w_ref, qg_ref, kg_ref, zg_ref, bd_ref, ws_ref, bs_ref,
                     q_out, kt_out, v_out, g_out):
    tt = x_ref.shape[0]
    h = _rms_norm(x_ref[...], ng_ref[...]).astype(BF16)
    proj = _dot(h, w_ref[...])
    q = proj[:, 0:SB_WIDTH]
    k = proj[:, SB_WIDTH:2 * SB_WIDTH]
    v = proj[:, 2 * SB_WIDTH:3 * SB_WIDTH]
    u = proj[:, 3 * SB_WIDTH:3 * SB_WIDTH + SG_WIDTH]
    z = proj[:, 3 * SB_WIDTH + SG_WIDTH:]
    bd = bd_ref[...]

    def group_norm(t, g):
        t2 = (t * t).astype(BF16)
        ms = jnp.concatenate(
            [_dot(t2[:, j:j + MXU_TILE], bd) for j in range(0, t.shape[1], MXU_TILE)], axis=1)
        return t * lax.rsqrt(ms * np.float32(1.0 / HEAD_DIM) + EPS) * g

    q_out[...] = (group_norm(q, qg_ref[...]) * np.float32(HEAD_DIM ** -0.5 * np.log2(np.e))).astype(BF16)
    kn = group_norm(k, kg_ref[...])
    for j in range(tt // K_BLOCK):
        kt_out[j] = kn[j * K_BLOCK:(j + 1) * K_BLOCK, :].T.astype(BF16)
    v_out[...] = v.astype(BF16)

    ug = _gelu(u)
    zn = group_norm(_gelu(z), zg_ref[...]).astype(BF16)
    row = lax.broadcasted_iota(jnp.int32, (CHUNK, CHUNK), 0)
    col = lax.broadcasted_iota(jnp.int32, (CHUNK, CHUNK), 1)
    n_ch = tt // CHUNK
    lane = lax.broadcasted_iota(jnp.int32, (CHUNK, n_ch * LANES), 1)
    low = (lane % LANES) < SG_GROUP
    wm = [jnp.where(col <= row, ws_ref[g], 0.0).astype(BF16) for g in range(SG_GROUPS)]
    bs = bs_ref[...]
    for p in range(SG_WIDTH // LANES):
        cols = slice(p * LANES, (p + 1) * LANES)
        zp = jnp.concatenate([zn[c * CHUNK:(c + 1) * CHUNK, cols] for c in range(n_ch)], axis=1)
        rhs = jnp.concatenate(
            [jnp.where(low, zp, jnp.zeros_like(zp)), jnp.where(low, jnp.zeros_like(zp), zp)], axis=0)
        s = _dot(jnp.concatenate([wm[2 * p], wm[2 * p + 1]], axis=1), rhs)
        for c in range(n_ch):
            rows = slice(c * CHUNK, (c + 1) * CHUNK)
            g_out[rows, cols] = (ug[rows, cols] * (s[:, c * LANES:(c + 1) * LANES] + bs[:, cols])).astype(BF16)


def _mixer_in(x2, ng, w_in, qg, kg, zg, bd, ws, bs):
    n, d = x2.shape
    tt = BIG_TILE
    in_w = w_in.shape[1]
    return pl.pallas_call(
        _mixer_in_kernel,
        grid=(n // tt,),
        in_specs=[
            pl.BlockSpec((tt, d), lambda i: (i, 0)),
            _const_spec((1, d)),
            _const_spec((d, in_w)),
            _const_spec((1, SB_WIDTH)),
            _const_spec((1, SB_WIDTH)),
            _const_spec((1, SG_WIDTH)),
            _const_spec((MXU_TILE, MXU_TILE)),
            _const_spec((SG_GROUPS, CHUNK, CHUNK)),
            _const_spec((CHUNK, SG_WIDTH)),
        ],
        out_specs=[
            pl.BlockSpec((tt, SB_WIDTH), lambda i: (i, 0)),
            pl.BlockSpec((tt // K_BLOCK, SB_WIDTH, K_BLOCK), lambda i: (i, 0, 0)),
            pl.BlockSpec((tt, SB_WIDTH), lambda i: (i, 0)),
            pl.BlockSpec((tt, SG_WIDTH), lambda i: (i, 0)),
        ],
        out_shape=[
            jax.ShapeDtypeStruct((n, SB_WIDTH), BF16),
            jax.ShapeDtypeStruct((n // K_BLOCK, SB_WIDTH, K_BLOCK), BF16),
            jax.ShapeDtypeStruct((n, SB_WIDTH), BF16),
            jax.ShapeDtypeStruct((n, SG_WIDTH), BF16),
        ],
        compiler_params=_params("parallel"),
        name="mixer_in",
    )(x2, ng, w_in, qg, kg, zg, bd, ws, bs)


def _attn_kernel(q_ref, kt_ref, v_ref, tri_ref, o_ref, qm_ref, acc_ref, car_ref):
    for sub in range(Q_PER_STEP):
        rows = slice(sub * Q_BLOCK, (sub + 1) * Q_BLOCK)
        _attn_query_block(pl.program_id(1) * Q_PER_STEP + sub, q_ref.at[rows], kt_ref, v_ref, tri_ref,
                          o_ref.at[rows], qm_ref, acc_ref, car_ref)


def _attn_query_block(i, q_ref, kt_ref, v_ref, tri_ref, o_ref, qm_ref, acc_ref, car_ref):
    lane = lax.broadcasted_iota(jnp.int32, (Q_BLOCK, LANES), 1)
    row = lax.broadcasted_iota(jnp.int32, (Q_BLOCK, K_BLOCK), 0)
    col = lax.broadcasted_iota(jnp.int32, (Q_BLOCK, K_BLOCK), 1)
    causal = col < row
    low = lane < HEAD_DIM

    for h in range(N_HEADS):
        qp = q_ref[:, (h // 2) * LANES:(h // 2 + 1) * LANES]
        keep = low if h % 2 == 0 else jnp.logical_not(low)
        qm_ref[h] = jnp.where(keep, qp, jnp.zeros_like(qp))
    acc_ref[...] = jnp.zeros_like(acc_ref)
    car_ref[...] = jnp.zeros_like(car_ref)

    def block(kb, diag):
        k0 = pl.multiple_of(kb * K_BLOCK, K_BLOCK)
        grp = [slice((h // 2) * LANES, (h // 2 + 1) * LANES) for h in range(N_HEADS)]
        zs = [_dot(qm_ref[h], kt_ref[kb, grp[h], :]) for h in range(N_HEADS)]
        args = []
        for h in range(N_HEADS):
            z = zs[h]
            neg_abs = lax.bitcast_convert_type(
                lax.bitcast_convert_type(z, jnp.uint32) | jnp.uint32(0x80000000), F32)
            sp = jnp.log2(1.0 + jnp.exp2(neg_abs))
            ls = jnp.minimum(z, 0.0) - sp
            lk = ls - z
            if diag:
                lk = jnp.where(causal, lk, 0.0)
            car = car_ref[h]
            tail = _dot(lk.astype(BF16), tri_ref[...])
            args.append(ls + tail + jnp.concatenate([car, car], axis=1))
            car_ref[h] = car + jnp.sum(lk, axis=1, keepdims=True)
        for h in range(N_HEADS):
            w = jnp.exp2(args[h])
            if diag:
                w = jnp.where(causal, w, 0.0)
            acc_ref[h] += _dot(w.astype(BF16), v_ref[pl.ds(k0, K_BLOCK), grp[h]])

    def max_carry():
        m = car_ref[0]
        for h in range(1, N_HEADS):
            m = jnp.maximum(m, car_ref[h])
        return jnp.max(m)

    block(i, True)

    def cond(state):
        kb, mx = state
        return jnp.logical_and(kb >= 0, mx > DEAD_LOG2)

    def body(state):
        kb, _ = state
        block(kb, False)
        return kb - 1, max_carry()

    lax.while_loop(cond, body, (i - 1, max_carry()))

    for p in range(N_HEADS // 2):
        o_ref[:, p * LANES:(p + 1) * LANES] = jnp.where(
            low, acc_ref[2 * p], acc_ref[2 * p + 1]).astype(o_ref.dtype)


def _attention(q, kt, v, tri, batch, seq):
    qs = Q_BLOCK * Q_PER_STEP
    nq = seq // qs
    nk = seq // K_BLOCK
    return pl.pallas_call(
        _attn_kernel,
        grid=(batch, nq),
        in_specs=[
            pl.BlockSpec((qs, SB_WIDTH), lambda b, i: (b * nq + i, 0)),
            pl.BlockSpec((nk, SB_WIDTH, K_BLOCK), lambda b, i: (b, 0, 0), pipeline_mode=pl.Buffered(1)),
            pl.BlockSpec((seq, SB_WIDTH), lambda b, i: (b, 0), pipeline_mode=pl.Buffered(1)),
            _const_spec((K_BLOCK, K_BLOCK)),
        ],
        out_specs=pl.BlockSpec((qs, SB_WIDTH), lambda b, i: (b * nq + i, 0)),
        out_shape=jax.ShapeDtypeStruct((batch * seq, SB_WIDTH), BF16),
        scratch_shapes=[
            pltpu.VMEM((N_HEADS, Q_BLOCK, LANES), BF16),
            pltpu.VMEM((N_HEADS, Q_BLOCK, LANES), F32),
            pltpu.VMEM((N_HEADS, Q_BLOCK, LANES), F32),
        ],
        compiler_params=_params("parallel", "parallel"),
        name="sb_attention",
    )(q, kt, v, tri)


def _ffn_kernel(tiles_per_seq, has_mix, *refs):
    if has_mix:
        x_ref, a_ref, g_ref, wa_ref, wo_ref = refs[:5]
        refs = (x_ref,) + refs[5:]
    x_ref, ng_ref, wu_ref, wdw_ref, bdw_ref, wd_ref, o_ref, cbuf_ref, act_ref = refs
    t = pl.program_id(0)
    tt = x_ref.shape[0]
    d_ff = wd_ref.shape[0]
    n_chunks = d_ff // FF_CHUNK

    @pl.when(t == 0)
    def _():
        cbuf_ref[...] = jnp.zeros_like(cbuf_ref)

    first = (t % tiles_per_seq) == 0
    x = x_ref[...]
    if has_mix:
        x = x + _dot(a_ref[...], wa_ref[...]) + _dot(g_ref[...], wo_ref[...])
    h = _rms_norm(x, ng_ref[...]).astype(BF16)
    for c in range(n_chunks):
        gate = _dot(h, wu_ref[:, c * FF_CHUNK:(c + 1) * FF_CHUNK])
        val = _dot(h, wu_ref[:, d_ff + c * FF_CHUNK:d_ff + (c + 1) * FF_CHUNK])
        wk = wdw_ref[c]
        bias = bdw_ref[c]
        for j in range(FF_CHUNK // LANES):
            b = c * (FF_CHUNK // LANES) + j
            cols = slice(j * LANES, (j + 1) * LANES)
            gj = gate[:, cols]
            prev = jnp.where(first, 0.0, cbuf_ref[b, tt:tt + FF_HALO, :])
            cbuf_ref[b, 0:FF_HALO, :] = prev
            cbuf_ref[b, FF_HALO:FF_HALO + tt, :] = gj
            conv = bias[:, cols] + gj * wk[FFN_K - 1:FFN_K, cols]
            for k in range(FFN_K - 1):
                off = FF_HALO - (FFN_K - 1) + k
                conv = conv + cbuf_ref[b, off:off + tt, :] * wk[k:k + 1, cols]
            act = conv * jax.nn.sigmoid(conv) * val[:, cols]
            act_ref[:, c * FF_CHUNK + j * LANES:c * FF_CHUNK + (j + 1) * LANES] = act.astype(BF16)
    o_ref[...] = x + _dot(act_ref[...], wd_ref[...])


def _ffn(x2, mix, ng, wu, wdw, bdw, wd, seq):
    n, d = x2.shape
    tt = TOK_TILE
    nc = wd.shape[0] // FF_CHUNK

    def tok(width):
        return pl.BlockSpec((tt, width), lambda i: (i, 0))

    mix_specs = [] if mix is None else [
        tok(SB_WIDTH), tok(SG_WIDTH), _const_spec((SB_WIDTH, d)), _const_spec((SG_WIDTH, d))]
    return pl.pallas_call(
        functools.partial(_ffn_kernel, seq // tt, mix is not None),
        grid=(n // tt,),
        in_specs=[tok(d)] + mix_specs + [
            _const_spec((1, d)),
            _const_spec((d, 2 * nc * FF_CHUNK)),
            _const_spec((nc, 8, FF_CHUNK)),
            _const_spec((nc, 1, FF_CHUNK)),
            _const_spec((nc * FF_CHUNK, d)),
        ],
        out_specs=pl.BlockSpec((tt, d), lambda i: (i, 0)),
        out_shape=jax.ShapeDtypeStruct((n, d), F32),
        scratch_shapes=[
            pltpu.VMEM((nc * FF_CHUNK // LANES, tt + FF_HALO, LANES), F32),
            pltpu.VMEM((tt, nc * FF_CHUNK), BF16),
        ],
        compiler_params=_params("arbitrary"),
        name="conv_glu_ffn",
    )(x2, *(() if mix is None else mix), ng, wu, wdw, bdw, wd)


def _conformer_kernel(tiles_per_seq, x_ref, ng_ref, w1_ref, b1_ref, wdw_ref, bdw_ref, lng_ref, lnb_ref,
                      w2_ref, b2_ref, o_ref, ybuf_ref, cv_ref, act_ref):
    t = pl.program_id(0)
    tt = x_ref.shape[0]
    cw = w1_ref.shape[1] // 2
    n_chunks = cw // CV_CHUNK

    @pl.when(t == 0)
    def _():
        ybuf_ref[...] = jnp.zeros_like(ybuf_ref)

    first = (t % tiles_per_seq) == 0
    x = x_ref[...]
    h = _rms_norm(x, ng_ref[...]).astype(BF16)
    pre = _dot(h, w1_ref[...]) + b1_ref[...]
    y = pre[:, :cw] * jax.nn.sigmoid(pre[:, cw:])
    for c in range(n_chunks):
        prev = jnp.where(first, 0.0, ybuf_ref[c, tt:tt + CV_HALO, :])
        ybuf_ref[c, CV_HALO:CV_HALO + tt, :] = y[:, c * CV_CHUNK:(c + 1) * CV_CHUNK]
        ybuf_ref[c, 0:CV_HALO, :] = prev

    def conv_chunk(c, carry):
        bias = bdw_ref[c]
        for rb in range(tt // CV_ROWS):
            r0 = rb * CV_ROWS
            acc = jnp.broadcast_to(bias, (CV_ROWS, CV_CHUNK))
            for k in range(CONV_K):
                off = CV_HALO - (CONV_K - 1) + k + r0
                acc = acc + ybuf_ref[c, off:off + CV_ROWS, :] * wdw_ref[c, k:k + 1, :]
            cv_ref[c, r0:r0 + CV_ROWS, :] = acc
        return carry

    lax.fori_loop(0, n_chunks, conv_chunk, 0)

    tot = cv_ref[0]
    for c in range(1, n_chunks):
        tot = tot + cv_ref[c]
    mu = tot.sum(axis=-1, keepdims=True) * np.float32(1.0 / cw)
    sq = jnp.zeros_like(tot)
    for c in range(n_chunks):
        dc = cv_ref[c] - mu
        sq = sq + dc * dc
    rstd = lax.rsqrt(sq.sum(axis=-1, keepdims=True) * np.float32(1.0 / cw) + EPS)
    for c in range(n_chunks):
        ln = (cv_ref[c] - mu) * rstd * lng_ref[c] + lnb_ref[c]
        act_ref[:, c * CV_CHUNK:(c + 1) * CV_CHUNK] = (ln * jax.nn.sigmoid(ln)).astype(BF16)
    o_ref[...] = x + b2_ref[...] + _dot(act_ref[...], w2_ref[...])


def _conformer(x2, ng, w1, b1, wdw, bdw, lng, lnb, w2, b2, seq):
    n, d = x2.shape
    tt = BIG_TILE
    cw = w1.shape[1] // 2
    nc = cw // CV_CHUNK
    return pl.pallas_call(
        functools.partial(_conformer_kernel, seq // tt),
        grid=(n // tt,),
        in_specs=[
            pl.BlockSpec((tt, d), lambda i: (i, 0)),
            _const_spec((1, d)),
            _const_spec((d, 2 * cw)),
            _const_spec((1, 2 * cw)),
            _const_spec((nc, 32, CV_CHUNK)),
            _const_spec((nc, 1, CV_CHUNK)),
            _const_spec((nc, 1, CV_CHUNK)),
            _const_spec((nc, 1, CV_CHUNK)),
            _const_spec((cw, d)),
            _const_spec((1, d)),
        ],
        out_specs=pl.BlockSpec((tt, d), lambda i: (i, 0)),
        out_shape=jax.ShapeDtypeStruct((n, d), F32),
        scratch_shapes=[
            pltpu.VMEM((nc, tt + CV_HALO, CV_CHUNK), F32),
            pltpu.VMEM((nc, tt, CV_CHUNK), F32),
            pltpu.VMEM((tt, cw), BF16),
        ],
        compiler_params=_params("arbitrary"),
        name="conformer",
    )(x2, ng, w1, b1, wdw, bdw, lng, lnb, w2, b2)


def _chunk_cols(w, width):
    lead = w.shape[:-1]
    n = w.shape[-1] // width
    return jnp.moveaxis(w.reshape(lead + (n, width)), -2, 0)


def _pad_rows(w, rows):
    return jnp.pad(w, ((0, 0), (0, rows - w.shape[1]), (0, 0)))


def kernel(x, mix_norm_g, sb_w_in, sb_q_norm_g, sb_k_norm_g, sg_z_norm_g, sg_w_spatial, sg_b_spatial,
           hyb_w_out, cv_w_pw1, cv_b_pw1, cv_w_dw, cv_b_dw, cv_ln_g, cv_ln_b, cv_w_pw2, cv_b_pw2,
           ffn_norm_g, ffn_w_up, ffn_w_dw, ffn_b_dw, ffn_w_down):
    batch, seq, d = x.shape
    depth = mix_norm_g.shape[0]
    d_ff = ffn_w_down.shape[1]
    x2 = x.reshape(batch * seq, d)

    gid = np.arange(MXU_TILE) // HEAD_DIM
    bd = jnp.asarray(gid[:, None] == gid[None, :], BF16)
    kk = np.arange(K_BLOCK)
    tri = jnp.asarray(kk[:, None] > kk[None, :], BF16)

    for i in range(depth):
        j = i // 2
        ng = mix_norm_g[i].reshape(1, d)
        if i % 2 == 0:
            qg = jnp.tile(sb_q_norm_g[j], N_HEADS).reshape(1, SB_WIDTH)
            kg = jnp.tile(sb_k_norm_g[j], N_HEADS).reshape(1, SB_WIDTH)
            zg = sg_z_norm_g[j].reshape(1, SG_WIDTH)
            bs = jnp.repeat(sg_b_spatial[j].T, SG_GROUP, axis=1)
            q, kt, v, g = _mixer_in(x2, ng, sb_w_in[j].astype(BF16), qg, kg, zg, bd, sg_w_spatial[j], bs)
            a = _attention(q, kt, v, tri, batch, seq)
            w_out = hyb_w_out[j].astype(BF16)
            mix = (a, g, w_out[:SB_WIDTH], w_out[SB_WIDTH:])
        else:
            cw = cv_w_dw.shape[2]
            x2 = _conformer(
                x2, ng, cv_w_pw1[j].astype(BF16), cv_b_pw1[j].reshape(1, 2 * cw),
                _pad_rows(_chunk_cols(cv_w_dw[j], CV_CHUNK), 32),
                _chunk_cols(cv_b_dw[j].reshape(1, cw), CV_CHUNK),
                _chunk_cols(cv_ln_g[j].reshape(1, cw), CV_CHUNK),
                _chunk_cols(cv_ln_b[j].reshape(1, cw), CV_CHUNK),
                cv_w_pw2[j].astype(BF16),
                cv_b_pw2[j].reshape(1, d), seq)
            mix = None
        x2 = _ffn(
            x2, mix, ffn_norm_g[i].reshape(1, d), ffn_w_up[i].astype(BF16),
            _pad_rows(_chunk_cols(ffn_w_dw[i], FF_CHUNK), 8),
            _chunk_cols(ffn_b_dw[i].reshape(1, d_ff), FF_CHUNK),
            ffn_w_down[i].astype(BF16), seq)
    return x2.reshape(batch, seq, d)
```

```python
import functools

import jax
import jax.numpy as jnp
import numpy as np
from jax import lax
from jax.experimental import pallas as pl
from jax.experimental.pallas import tpu as pltpu

F32 = jnp.float32
BF16 = jnp.bfloat16
EPS = 1e-6

HEAD_DIM = 64
N_HEADS = 8
SB_WIDTH = HEAD_DIM * N_HEADS
SG_GROUP = 64
SG_GROUPS = 8
SG_WIDTH = SG_GROUP * SG_GROUPS
CHUNK = 128
CONV_K = 31
FFN_K = 3

LANES = 128
TOK_TILE = 512
BIG_TILE = 1024
Q_BLOCK = 256
Q_PER_STEP = 2
MXU_TILE = 256
K_BLOCK = MXU_TILE
FF_CHUNK = 256
CV_CHUNK = 128
CV_ROWS = 32
CV_HALO = 32
FF_HALO = 8
VMEM_LIMIT = 56 * 1024 * 1024
DEAD_LOG2 = -160.0


def _dot(a, b):
    return jnp.dot(a, b, preferred_element_type=F32)


def _rms_norm(x, g):
    return x * lax.rsqrt(jnp.mean(x * x, axis=-1, keepdims=True) + EPS) * g


def _gelu(t):
    return 0.5 * t * (1.0 + lax.erf(t * np.float32(0.7071067811865476)))


def _split_dot(t, m):
    hi = t.astype(BF16)
    lo = (t - hi.astype(F32)).astype(BF16)
    return _dot(hi, m) + _dot(lo, m)


def _const_spec(shape):
    nd = len(shape)
    return pl.BlockSpec(shape, lambda *_: (0,) * nd, pipeline_mode=pl.Buffered(1))


def _params(*sem):
    return pltpu.CompilerParams(dimension_semantics=sem, vmem_limit_bytes=VMEM_LIMIT)


def _mixer_in_kernel(x_ref, ng_ref, w_ref, qg_ref, kg_ref, zg_ref, bd_ref, ws_ref, bs_ref,
                     q_out, kt_out, v_out, g_out):
    tt = x_ref.shape[0]
    h = _rms_norm(x_ref[...], ng_ref[...]).astype(BF16)
    proj = _dot(h, w_ref[...])
    q = proj[:, 0:SB_WIDTH]
    k = proj[:, SB_WIDTH:2 * SB_WIDTH]
    v = proj[:, 2 * SB_WIDTH:3 * SB_WIDTH]
    u = proj[:, 3 * SB_WIDTH:3 * SB_WIDTH + SG_WIDTH]
    z = proj[:, 3 * SB_WIDTH + SG_WIDTH:]
    bd = bd_ref[...]

    def group_norm(t, g):
        t2 = (t * t).astype(BF16)
        ms = jnp.concatenate(
            [_dot(t2[:, j:j + MXU_TILE], bd) for j in range(0, t.shape[1], MXU_TILE)], axis=1)
        return t * lax.rsqrt(ms * np.float32(1.0 / HEAD_DIM) + EPS) * g

    q_out[...] = (group_norm(q, qg_ref[...]) * np.float32(HEAD_DIM ** -0.5 * np.log2(np.e))).astype(BF16)
    kn = group_norm(k, kg_ref[...])
    for j in range(tt // K_BLOCK):
        kt_out[j] = kn[j * K_BLOCK:(j + 1) * K_BLOCK, :].T.astype(BF16)
    v_out[...] = v.astype(BF16)

    ug = _gelu(u)
    zn = group_norm(_gelu(z), zg_ref[...]).astype(BF16)
    row = lax.broadcasted_iota(jnp.int32, (CHUNK, CHUNK), 0)
    col = lax.broadcasted_iota(jnp.int32, (CHUNK, CHUNK), 1)
    n_ch = tt // CHUNK
    lane = lax.broadcasted_iota(jnp.int32, (CHUNK, n_ch * LANES), 1)
    low = (lane % LANES) < SG_GROUP
    wm = [jnp.where(col <= row, ws_ref[g], 0.0).astype(BF16) for g in range(SG_GROUPS)]
    bs = bs_ref[...]
    for p in range(SG_WIDTH // LANES):
        cols = slice(p * LANES, (p + 1) * LANES)
        zp = jnp.concatenate([zn[c * CHUNK:(c + 1) * CHUNK, cols] for c in range(n_ch)], axis=1)
        rhs = jnp.concatenate(
            [jnp.where(low, zp, jnp.zeros_like(zp)), jnp.where(low, jnp.zeros_like(zp), zp)], axis=0)
        s = _dot(jnp.concatenate([wm[2 * p], wm[2 * p + 1]], axis=1), rhs)
        for c in range(n_ch):
            rows = slice(c * CHUNK, (c + 1) * CHUNK)
            g_out[rows, cols] = (ug[rows, cols] * (s[:, c * LANES:(c + 1) * LANES] + bs[:, cols])).astype(BF16)


def _mixer_in(x2, ng, w_in, qg, kg, zg, bd, ws, bs):
    n, d = x2.shape
    tt = BIG_TILE
    in_w = w_in.shape[1]
    return pl.pallas_call(
        _mixer_in_kernel,
        grid=(n // tt,),
        in_specs=[
            pl.BlockSpec((tt, d), lambda i: (i, 0)),
            _const_spec((1, d)),
            _const_spec((d, in_w)),
            _const_spec((1, SB_WIDTH)),
            _const_spec((1, SB_WIDTH)),
            _const_spec((1, SG_WIDTH)),
            _const_spec((MXU_TILE, MXU_TILE)),
            _const_spec((SG_GROUPS, CHUNK, CHUNK)),
            _const_spec((CHUNK, SG_WIDTH)),
        ],
        out_specs=[
            pl.BlockSpec((tt, SB_WIDTH), lambda i: (i, 0)),
            pl.BlockSpec((tt // K_BLOCK, SB_WIDTH, K_BLOCK), lambda i: (i, 0, 0)),
            pl.BlockSpec((tt, SB_WIDTH), lambda i: (i, 0)),
            pl.BlockSpec((tt, SG_WIDTH), lambda i: (i, 0)),
        ],
        out_shape=[
            jax.ShapeDtypeStruct((n, SB_WIDTH), BF16),
            jax.ShapeDtypeStruct((n // K_BLOCK, SB_WIDTH, K_BLOCK), BF16),
            jax.ShapeDtypeStruct((n, SB_WIDTH), BF16),
            jax.ShapeDtypeStruct((n, SG_WIDTH), BF16),
        ],
        compiler_params=_params("parallel"),
        name="mixer_in",
    )(x2, ng, w_in, qg, kg, zg, bd, ws, bs)


def _attn_kernel(q_ref, kt_ref, v_ref, tri_ref, o_ref, qm_ref, acc_ref, car_ref):
    for sub in range(Q_PER_STEP):
        rows = slice(sub * Q_BLOCK, (sub + 1) * Q_BLOCK)
        _attn_query_block(pl.program_id(1) * Q_PER_STEP + sub, q_ref.at[rows], kt_ref, v_ref, tri_ref,
                          o_ref.at[rows], qm_ref, acc_ref, car_ref)


def _attn_query_block(i, q_ref, kt_ref, v_ref, tri_ref, o_ref, qm_ref, acc_ref, car_ref):
    lane = lax.broadcasted_iota(jnp.int32, (Q_BLOCK, LANES), 1)
    row = lax.broadcasted_iota(jnp.int32, (Q_BLOCK, K_BLOCK), 0)
    col = lax.broadcasted_iota(jnp.int32, (Q_BLOCK, K_BLOCK), 1)
    causal = col < row
    low = lane < HEAD_DIM

    for h in range(N_HEADS):
        qp = q_ref[:, (h // 2) * LANES:(h // 2 + 1) * LANES]
        keep = low if h % 2 == 0 else jnp.logical_not(low)
        qm_ref[h] = jnp.where(keep, qp, jnp.zeros_like(qp))

    def block(kb, diag):
        k0 = pl.multiple_of(kb * K_BLOCK, K_BLOCK)
        grp = [slice((h // 2) * LANES, (h // 2 + 1) * LANES) for h in range(N_HEADS)]
        zs = [_dot(qm_ref[h], kt_ref[kb, grp[h], :]) for h in range(N_HEADS)]
        args = []
        for h in range(N_HEADS):
            z = zs[h]
            neg_abs = lax.bitcast_convert_type(
                lax.bitcast_convert_type(z, jnp.uint32) | jnp.uint32(0x80000000), F32)
            sp = jnp.log2(1.0 + jnp.exp2(neg_abs))
            ls = jnp.minimum(z, 0.0) - sp
            lk = ls - z
            if diag:
                lk = jnp.where(causal, lk, 0.0)
            tail = _dot(lk.astype(BF16), tri_ref[...])
            row_total = jnp.broadcast_to(jnp.sum(lk, axis=1, keepdims=True), (Q_BLOCK, LANES))
            if diag:
                args.append(ls + tail)
                car_ref[h] = row_total
            else:
                car = car_ref[h]
                args.append(ls + tail + jnp.concatenate([car, car], axis=1))
                car_ref[h] = car + row_total
        for h in range(N_HEADS):
            w = jnp.exp2(args[h])
            if diag:
                w = jnp.where(causal, w, 0.0)
            pv = _dot(w.astype(BF16), v_ref[pl.ds(k0, K_BLOCK), grp[h]])
            if diag:
                acc_ref[h] = pv
            else:
                acc_ref[h] += pv

    def max_carry():
        m = car_ref[0]
        for h in range(1, N_HEADS):
            m = jnp.maximum(m, car_ref[h])
        return jnp.max(m)

    block(i, True)

    def cond(state):
        kb, mx = state
        return jnp.logical_and(kb >= 0, mx > DEAD_LOG2)

    def body(state):
        kb, _ = state
        block(kb, False)
        return kb - 1, max_carry()

    lax.while_loop(cond, body, (i - 1, jnp.float32(0.0)))

    for p in range(N_HEADS // 2):
        o_ref[:, p * LANES:(p + 1) * LANES] = jnp.where(
            low, acc_ref[2 * p], acc_ref[2 * p + 1]).astype(o_ref.dtype)


def _attention(q, kt, v, tri, batch, seq):
    qs = Q_BLOCK * Q_PER_STEP
    nq = seq // qs
    nk = seq // K_BLOCK
    return pl.pallas_call(
        _attn_kernel,
        grid=(batch, nq),
        in_specs=[
            pl.BlockSpec((qs, SB_WIDTH), lambda b, i: (b * nq + i, 0)),
            pl.BlockSpec((nk, SB_WIDTH, K_BLOCK), lambda b, i: (b, 0, 0), pipeline_mode=pl.Buffered(1)),
            pl.BlockSpec((seq, SB_WIDTH), lambda b, i: (b, 0), pipeline_mode=pl.Buffered(1)),
            _const_spec((K_BLOCK, K_BLOCK)),
        ],
        out_specs=pl.BlockSpec((qs, SB_WIDTH), lambda b, i: (b * nq + i, 0)),
        out_shape=jax.ShapeDtypeStruct((batch * seq, SB_WIDTH), BF16),
        scratch_shapes=[
            pltpu.VMEM((N_HEADS, Q_BLOCK, LANES), BF16),
            pltpu.VMEM((N_HEADS, Q_BLOCK, LANES), F32),
            pltpu.VMEM((N_HEADS, Q_BLOCK, LANES), F32),
        ],
        compiler_params=_params("parallel", "parallel"),
        name="sb_attention",
    )(q, kt, v, tri)


def _ffn_kernel(tiles_per_seq, has_mix, *refs):
    if has_mix:
        x_ref, a_ref, g_ref, wa_ref, wo_ref = refs[:5]
        refs = (x_ref,) + refs[5:]
    x_ref, ng_ref, wu_ref, wdw_ref, bdw_ref, wd_ref, o_ref, cbuf_ref, act_ref = refs
    t = pl.program_id(0)
    tt = x_ref.shape[0]
    d_ff = wd_ref.shape[0]
    n_chunks = d_ff // FF_CHUNK

    @pl.when(t == 0)
    def _():
        cbuf_ref[...] = jnp.zeros_like(cbuf_ref)

    first = (t % tiles_per_seq) == 0
    x = x_ref[...]
    if has_mix:
        x = x + _dot(a_ref[...], wa_ref[...]) + _dot(g_ref[...], wo_ref[...])
    h = _rms_norm(x, ng_ref[...]).astype(BF16)
    for c in range(n_chunks):
        gate = _dot(h, wu_ref[:, c * FF_CHUNK:(c + 1) * FF_CHUNK])
        val = _dot(h, wu_ref[:, d_ff + c * FF_CHUNK:d_ff + (c + 1) * FF_CHUNK])
        wk = wdw_ref[c]
        bias = bdw_ref[c]
        for j in range(FF_CHUNK // LANES):
            b = c * (FF_CHUNK // LANES) + j
            cols = slice(j * LANES, (j + 1) * LANES)
            gj = gate[:, cols]
            prev = jnp.where(first, 0.0, cbuf_ref[b, tt:tt + FF_HALO, :])
            cbuf_ref[b, 0:FF_HALO, :] = prev
            cbuf_ref[b, FF_HALO:FF_HALO + tt, :] = gj
            conv = bias[:, cols] + gj * wk[FFN_K - 1:FFN_K, cols]
            for k in range(FFN_K - 1):
                off = FF_HALO - (FFN_K - 1) + k
                conv = conv + cbuf_ref[b, off:off + tt, :] * wk[k:k + 1, cols]
            act = conv * jax.nn.sigmoid(conv) * val[:, cols]
            act_ref[:, c * FF_CHUNK + j * LANES:c * FF_CHUNK + (j + 1) * LANES] = act.astype(BF16)
    o_ref[...] = x + _dot(act_ref[...], wd_ref[...])


def _ffn(x2, mix, ng, wu, wdw, bdw, wd, seq):
    n, d = x2.shape
    tt = TOK_TILE
    nc = wd.shape[0] // FF_CHUNK

    def tok(width):
        return pl.BlockSpec((tt, width), lambda i: (i, 0))

    mix_specs = [] if mix is None else [
        tok(SB_WIDTH), tok(SG_WIDTH), _const_spec((SB_WIDTH, d)), _const_spec((SG_WIDTH, d))]
    return pl.pallas_call(
        functools.partial(_ffn_kernel, seq // tt, mix is not None),
        grid=(n // tt,),
        in_specs=[tok(d)] + mix_specs + [
            _const_spec((1, d)),
            _const_spec((d, 2 * nc * FF_CHUNK)),
            _const_spec((nc, 8, FF_CHUNK)),
            _const_spec((nc, 1, FF_CHUNK)),
            _const_spec((nc * FF_CHUNK, d)),
        ],
        out_specs=pl.BlockSpec((tt, d), lambda i: (i, 0)),
        out_shape=jax.ShapeDtypeStruct((n, d), F32),
        scratch_shapes=[
            pltpu.VMEM((nc * FF_CHUNK // LANES, tt + FF_HALO, LANES), F32),
            pltpu.VMEM((tt, nc * FF_CHUNK), BF16),
        ],
        compiler_params=_params("arbitrary"),
        name="conv_glu_ffn",
    )(x2, *(() if mix is None else mix), ng, wu, wdw, bdw, wd)


def _conformer_kernel(tiles_per_seq, x_ref, ng_ref, w1_ref, b1_ref, wdw_ref, bdw_ref, lng_ref, lnb_ref,
                      w2_ref, b2_ref, o_ref, ybuf_ref, cv_ref, act_ref):
    t = pl.program_id(0)
    tt = x_ref.shape[0]
    cw = w1_ref.shape[1] // 2
    n_chunks = cw // CV_CHUNK

    @pl.when(t == 0)
    def _():
        ybuf_ref[...] = jnp.zeros_like(ybuf_ref)

    first = (t % tiles_per_seq) == 0
    x = x_ref[...]
    h = _rms_norm(x, ng_ref[...]).astype(BF16)
    pre = _dot(h, w1_ref[...]) + b1_ref[...]
    y = pre[:, :cw] * jax.nn.sigmoid(pre[:, cw:])
    for c in range(n_chunks):
        prev = jnp.where(first, 0.0, ybuf_ref[c, tt:tt + CV_HALO, :])
        ybuf_ref[c, CV_HALO:CV_HALO + tt, :] = y[:, c * CV_CHUNK:(c + 1) * CV_CHUNK]
        ybuf_ref[c, 0:CV_HALO, :] = prev

    def conv_chunk(c, carry):
        bias = bdw_ref[c]
        for rb in range(tt // CV_ROWS):
            r0 = rb * CV_ROWS
            acc = jnp.broadcast_to(bias, (CV_ROWS, CV_CHUNK))
            for k in range(CONV_K):
                off = CV_HALO - (CONV_K - 1) + k + r0
                acc = acc + ybuf_ref[c, off:off + CV_ROWS, :] * wdw_ref[c, k:k + 1, :]
            cv_ref[c, r0:r0 + CV_ROWS, :] = acc
        return carry

    lax.fori_loop(0, n_chunks, conv_chunk, 0)

    tot = cv_ref[0]
    for c in range(1, n_chunks):
        tot = tot + cv_ref[c]
    mu = tot.sum(axis=-1, keepdims=True) * np.float32(1.0 / cw)
    sq = jnp.zeros_like(tot)
    for c in range(n_chunks):
        dc = cv_ref[c] - mu
        sq = sq + dc * dc
    rstd = lax.rsqrt(sq.sum(axis=-1, keepdims=True) * np.float32(1.0 / cw) + EPS)
    for c in range(n_chunks):
        ln = (cv_ref[c] - mu) * rstd * lng_ref[c] + lnb_ref[c]
        act_ref[:, c * CV_CHUNK:(c + 1) * CV_CHUNK] = (ln * jax.nn.sigmoid(ln)).astype(BF16)
    o_ref[...] = x + b2_ref[...] + _dot(act_ref[...], w2_ref[...])


def _conformer(x2, ng, w1, b1, wdw, bdw, lng, lnb, w2, b2, seq):
    n, d = x2.shape
    tt = BIG_TILE
    cw = w1.shape[1] // 2
    nc = cw // CV_CHUNK
    return pl.pallas_call(
        functools.partial(_conformer_kernel, seq // tt),
        grid=(n // tt,),
        in_specs=[
            pl.BlockSpec((tt, d), lambda i: (i, 0)),
            _const_spec((1, d)),
            _const_spec((d, 2 * cw)),
            _const_spec((1, 2 * cw)),
            _const_spec((nc, 32, CV_CHUNK)),
            _const_spec((nc, 1, CV_CHUNK)),
            _const_spec((nc, 1, CV_CHUNK)),
            _const_spec((nc, 1, CV_CHUNK)),
            _const_spec((cw, d)),
            _const_spec((1, d)),
        ],
        out_specs=pl.BlockSpec((tt, d), lambda i: (i, 0)),
        out_shape=jax.ShapeDtypeStruct((n, d), F32),
        scratch_shapes=[
            pltpu.VMEM((nc, tt + CV_HALO, CV_CHUNK), F32),
            pltpu.VMEM((nc, tt, CV_CHUNK), F32),
            pltpu.VMEM((tt, cw), BF16),
        ],
        compiler_params=_params("arbitrary"),
        name="conformer",
    )(x2, ng, w1, b1, wdw, bdw, lng, lnb, w2, b2)


def _chunk_cols(w, width):
    lead = w.shape[:-1]
    n = w.shape[-1] // width
    return jnp.moveaxis(w.reshape(lead + (n, width)), -2, 0)


def _pad_rows(w, rows):
    return jnp.pad(w, ((0, 0), (0, rows - w.shape[1]), (0, 0)))


def kernel(x, mix_norm_g, sb_w_in, sb_q_norm_g, sb_k_norm_g, sg_z_norm_g, sg_w_spatial, sg_b_spatial,
           hyb_w_out, cv_w_pw1, cv_b_pw1, cv_w_dw, cv_b_dw, cv_ln_g, cv_ln_b, cv_w_pw2, cv_b_pw2,
           ffn_norm_g, ffn_w_up, ffn_w_dw, ffn_b_dw, ffn_w_down):
    batch, seq, d = x.shape
    depth = mix_norm_g.shape[0]
    d_ff = ffn_w_down.shape[1]
    x2 = x.reshape(batch * seq, d)

    gid = np.arange(MXU_TILE) // HEAD_DIM
    bd = jnp.asarray(gid[:, None] == gid[None, :], BF16)
    kk = np.arange(K_BLOCK)
    tri = jnp.asarray(kk[:, None] > kk[None, :], BF16)

    for i in range(depth):
        j = i // 2
        ng = mix_norm_g[i].reshape(1, d)
        if i % 2 == 0:
            qg = jnp.tile(sb_q_norm_g[j], N_HEADS).reshape(1, SB_WIDTH)
            kg = jnp.tile(sb_k_norm_g[j], N_HEADS).reshape(1, SB_WIDTH)
            zg = sg_z_norm_g[j].reshape(1, SG_WIDTH)
            bs = jnp.repeat(sg_b_spatial[j].T, SG_GROUP, axis=1)
            q, kt, v, g = _mixer_in(x2, ng, sb_w_in[j].astype(BF16), qg, kg, zg, bd, sg_w_spatial[j], bs)
            a = _attention(q, kt, v, tri, batch, seq)
            w_out = hyb_w_out[j].astype(BF16)
            mix = (a, g, w_out[:SB_WIDTH], w_out[SB_WIDTH:])
        else:
            cw = cv_w_dw.shape[2]
            x2 = _conformer(
                x2, ng, cv_w_pw1[j].astype(BF16), cv_b_pw1[j].reshape(1, 2 * cw),
                _pad_rows(_chunk_cols(cv_w_dw[j], CV_CHUNK), 32),
                _chunk_cols(cv_b_dw[j].reshape(1, cw), CV_CHUNK),
                _chunk_cols(cv_ln_g[j].reshape(1, cw), CV_CHUNK),
                _chunk_cols(cv_ln_b[j].reshape(1, cw), CV_CHUNK),
                cv_w_pw2[j].astype(BF16),
                cv_b_pw2[j].reshape(1, d), seq)
            mix = None
        x2 = _ffn(
            x2, mix, ffn_norm_g[i].reshape(1, d), ffn_w_up[i].astype(BF16),
            _pad_rows(_chunk_cols(ffn_w_dw[i], FF_CHUNK), 8),
            _chunk_cols(ffn_b_dw[i].reshape(1, d_ff), FF_CHUNK),
            ffn_w_down[i].astype(BF16), seq)
    return x2.reshape(batch, seq, d)
```

```python
import functools

import jax
import jax.numpy as jnp
import numpy as np
from jax import lax
from jax.experimental import pallas as pl
from jax.experimental.pallas import tpu as pltpu

F32 = jnp.float32
BF16 = jnp.bfloat16
EPS = 1e-6

HEAD_DIM = 64
N_HEADS = 8
SB_WIDTH = HEAD_DIM * N_HEADS
SG_GROUP = 64
SG_GROUPS = 8
SG_WIDTH = SG_GROUP * SG_GROUPS
CHUNK = 128
CONV_K = 31
FFN_K = 3

LANES = 128
TOK_TILE = 512
BIG_TILE = 1024
Q_BLOCK = 256
Q_PER_STEP = 2
MXU_TILE = 256
K_BLOCK = MXU_TILE
FF_CHUNK = 256
DFT_N = MXU_TILE
CV_HALO = 32
FF_HALO = 8
VMEM_LIMIT = 56 * 1024 * 1024
DEAD_LOG2 = -160.0


def _dot(a, b):
    return jnp.dot(a, b, preferred_element_type=F32)


def _rms_norm(x, g):
    return x * lax.rsqrt(jnp.mean(x * x, axis=-1, keepdims=True) + EPS) * g


def _gelu(t):
    return 0.5 * t * (1.0 + lax.erf(t * np.float32(0.7071067811865476)))


def _split_dot(t, m):
    hi = t.astype(BF16)
    lo = (t - hi.astype(F32)).astype(BF16)
    return _dot(hi, m) + _dot(lo, m)


def _const_spec(shape):
    nd = len(shape)
    return pl.BlockSpec(shape, lambda *_: (0,) * nd, pipeline_mode=pl.Buffered(1))


def _params(*sem):
    return pltpu.CompilerParams(dimension_semantics=sem, vmem_limit_bytes=VMEM_LIMIT)


def _mixer_in_kernel(x_ref, ng_ref, w_ref, qg_ref, kg_ref, zg_ref, bd_ref, ws_ref, bs_ref,
                     q_out, kt_out, v_out, g_out):
    tt = x_ref.shape[0]
    h = _rms_norm(x_ref[...], ng_ref[...]).astype(BF16)
    proj = _dot(h, w_ref[...])
    q = proj[:, 0:SB_WIDTH]
    k = proj[:, SB_WIDTH:2 * SB_WIDTH]
    v = proj[:, 2 * SB_WIDTH:3 * SB_WIDTH]
    u = proj[:, 3 * SB_WIDTH:3 * SB_WIDTH + SG_WIDTH]
    z = proj[:, 3 * SB_WIDTH + SG_WIDTH:]
    bd = bd_ref[...]

    def group_norm(t, g):
        t2 = (t * t).astype(BF16)
        ms = jnp.concatenate(
            [_dot(t2[:, j:j + MXU_TILE], bd) for j in range(0, t.shape[1], MXU_TILE)], axis=1)
        return t * lax.rsqrt(ms * np.float32(1.0 / HEAD_DIM) + EPS) * g

    q_out[...] = (group_norm(q, qg_ref[...]) * np.float32(HEAD_DIM ** -0.5 * np.log2(np.e))).astype(BF16)
    kn = group_norm(k, kg_ref[...])
    for j in range(tt // K_BLOCK):
        kt_out[j] = kn[j * K_BLOCK:(j + 1) * K_BLOCK, :].T.astype(BF16)
    v_out[...] = v.astype(BF16)

    ug = _gelu(u)
    zn = group_norm(_gelu(z), zg_ref[...]).astype(BF16)
    row = lax.broadcasted_iota(jnp.int32, (CHUNK, CHUNK), 0)
    col = lax.broadcasted_iota(jnp.int32, (CHUNK, CHUNK), 1)
    n_ch = tt // CHUNK
    lane = lax.broadcasted_iota(jnp.int32, (CHUNK, n_ch * LANES), 1)
    low = (lane % LANES) < SG_GROUP
    wm = [jnp.where(col <= row, ws_ref[g], 0.0).astype(BF16) for g in range(SG_GROUPS)]
    bs = bs_ref[...]
    for p in range(SG_WIDTH // LANES):
        cols = slice(p * LANES, (p + 1) * LANES)
        zp = jnp.concatenate([zn[c * CHUNK:(c + 1) * CHUNK, cols] for c in range(n_ch)], axis=1)
        rhs = jnp.concatenate(
            [jnp.where(low, zp, jnp.zeros_like(zp)), jnp.where(low, jnp.zeros_like(zp), zp)], axis=0)
        s = _dot(jnp.concatenate([wm[2 * p], wm[2 * p + 1]], axis=1), rhs)
        for c in range(n_ch):
            rows = slice(c * CHUNK, (c + 1) * CHUNK)
            g_out[rows, cols] = (ug[rows, cols] * (s[:, c * LANES:(c + 1) * LANES] + bs[:, cols])).astype(BF16)


def _mixer_in(x2, ng, w_in, qg, kg, zg, bd, ws, bs):
    n, d = x2.shape
    tt = BIG_TILE
    in_w = w_in.shape[1]
    return pl.pallas_call(
        _mixer_in_kernel,
        grid=(n // tt,),
        in_specs=[
            pl.BlockSpec((tt, d), lambda i: (i, 0)),
            _const_spec((1, d)),
            _const_spec((d, in_w)),
            _const_spec((1, SB_WIDTH)),
            _const_spec((1, SB_WIDTH)),
            _const_spec((1, SG_WIDTH)),
            _const_spec((MXU_TILE, MXU_TILE)),
            _const_spec((SG_GROUPS, CHUNK, CHUNK)),
            _const_spec((CHUNK, SG_WIDTH)),
        ],
        out_specs=[
            pl.BlockSpec((tt, SB_WIDTH), lambda i: (i, 0)),
            pl.BlockSpec((tt // K_BLOCK, SB_WIDTH, K_BLOCK), lambda i: (i, 0, 0)),
            pl.BlockSpec((tt, SB_WIDTH), lambda i: (i, 0)),
            pl.BlockSpec((tt, SG_WIDTH), lambda i: (i, 0)),
        ],
        out_shape=[
            jax.ShapeDtypeStruct((n, SB_WIDTH), BF16),
            jax.ShapeDtypeStruct((n // K_BLOCK, SB_WIDTH, K_BLOCK), BF16),
            jax.ShapeDtypeStruct((n, SB_WIDTH), BF16),
            jax.ShapeDtypeStruct((n, SG_WIDTH), BF16),
        ],
        compiler_params=_params("parallel"),
        name="mixer_in",
    )(x2, ng, w_in, qg, kg, zg, bd, ws, bs)


def _attn_kernel(q_ref, kt_ref, v_ref, tri_ref, o_ref, qm_ref, acc_ref, car_ref):
    for sub in range(Q_PER_STEP):
        rows = slice(sub * Q_BLOCK, (sub + 1) * Q_BLOCK)
        _attn_query_block(pl.program_id(1) * Q_PER_STEP + sub, q_ref.at[rows], kt_ref, v_ref, tri_ref,
                          o_ref.at[rows], qm_ref, acc_ref, car_ref)


def _attn_query_block(i, q_ref, kt_ref, v_ref, tri_ref, o_ref, qm_ref, acc_ref, car_ref):
    lane = lax.broadcasted_iota(jnp.int32, (Q_BLOCK, LANES), 1)
    row = lax.broadcasted_iota(jnp.int32, (Q_BLOCK, K_BLOCK), 0)
    col = lax.broadcasted_iota(jnp.int32, (Q_BLOCK, K_BLOCK), 1)
    causal = col < row
    low = lane < HEAD_DIM

    for h in range(N_HEADS):
        qp = q_ref[:, (h // 2) * LANES:(h // 2 + 1) * LANES]
        keep = low if h % 2 == 0 else jnp.logical_not(low)
        qm_ref[h] = jnp.where(keep, qp, jnp.zeros_like(qp))

    def block(kb, diag):
        k0 = pl.multiple_of(kb * K_BLOCK, K_BLOCK)
        grp = [slice((h // 2) * LANES, (h // 2 + 1) * LANES) for h in range(N_HEADS)]
        zs = [_dot(qm_ref[h], kt_ref[kb, grp[h], :]) for h in range(N_HEADS)]
        args = []
        for h in range(N_HEADS):
            z = zs[h]
            neg_abs = lax.bitcast_convert_type(
                lax.bitcast_convert_type(z, jnp.uint32) | jnp.uint32(0x80000000), F32)
            sp = jnp.log2(1.0 + jnp.exp2(neg_abs))
            ls = jnp.minimum(z, 0.0) - sp
            lk = ls - z
            if diag:
                lk = jnp.where(causal, lk, 0.0)
            tail = _dot(lk.astype(BF16), tri_ref[...])
            row_total = jnp.broadcast_to(jnp.sum(lk, axis=1, keepdims=True), (Q_BLOCK, LANES))
            if diag:
                args.append(ls + tail)
                car_ref[h] = row_total
            else:
                car = car_ref[h]
                args.append(ls + tail + jnp.concatenate([car, car], axis=1))
                car_ref[h] = car + row_total
        for h in range(N_HEADS):
            w = jnp.exp2(args[h])
            if diag:
                w = jnp.where(causal, w, 0.0)
            pv = _dot(w.astype(BF16), v_ref[pl.ds(k0, K_BLOCK), grp[h]])
            if diag:
                acc_ref[h] = pv
            else:
                acc_ref[h] += pv

    def max_carry():
        m = car_ref[0]
        for h in range(1, N_HEADS):
            m = jnp.maximum(m, car_ref[h])
        return jnp.max(m)

    block(i, True)

    def cond(state):
        kb, mx = state
        return jnp.logical_and(kb >= 0, mx > DEAD_LOG2)

    def body(state):
        kb, _ = state
        block(kb, False)
        return kb - 1, max_carry()

    lax.while_loop(cond, body, (i - 1, jnp.float32(0.0)))

    for p in range(N_HEADS // 2):
        o_ref[:, p * LANES:(p + 1) * LANES] = jnp.where(
            low, acc_ref[2 * p], acc_ref[2 * p + 1]).astype(o_ref.dtype)


def _attention(q, kt, v, tri, batch, seq):
    qs = Q_BLOCK * Q_PER_STEP
    nq = seq // qs
    nk = seq // K_BLOCK
    return pl.pallas_call(
        _attn_kernel,
        grid=(batch, nq),
        in_specs=[
            pl.BlockSpec((qs, SB_WIDTH), lambda b, i: (b * nq + i, 0)),
            pl.BlockSpec((nk, SB_WIDTH, K_BLOCK), lambda b, i: (b, 0, 0), pipeline_mode=pl.Buffered(1)),
            pl.BlockSpec((seq, SB_WIDTH), lambda b, i: (b, 0), pipeline_mode=pl.Buffered(1)),
            _const_spec((K_BLOCK, K_BLOCK)),
        ],
        out_specs=pl.BlockSpec((qs, SB_WIDTH), lambda b, i: (b * nq + i, 0)),
        out_shape=jax.ShapeDtypeStruct((batch * seq, SB_WIDTH), BF16),
        scratch_shapes=[
            pltpu.VMEM((N_HEADS, Q_BLOCK, LANES), BF16),
            pltpu.VMEM((N_HEADS, Q_BLOCK, LANES), F32),
            pltpu.VMEM((N_HEADS, Q_BLOCK, LANES), F32),
        ],
        compiler_params=_params("parallel", "parallel"),
        name="sb_attention",
    )(q, kt, v, tri)


def _ffn_kernel(tiles_per_seq, has_mix, *refs):
    if has_mix:
        x_ref, a_ref, g_ref, wa_ref, wo_ref = refs[:5]
        refs = (x_ref,) + refs[5:]
    x_ref, ng_ref, wu_ref, wdw_ref, bdw_ref, wd_ref, o_ref, cbuf_ref, act_ref = refs
    t = pl.program_id(0)
    tt = x_ref.shape[0]
    d_ff = wd_ref.shape[0]
    n_chunks = d_ff // FF_CHUNK

    @pl.when(t == 0)
    def _():
        cbuf_ref[...] = jnp.zeros_like(cbuf_ref)

    first = (t % tiles_per_seq) == 0
    x = x_ref[...]
    if has_mix:
        x = x + _dot(a_ref[...], wa_ref[...]) + _dot(g_ref[...], wo_ref[...])
    h = _rms_norm(x, ng_ref[...]).astype(BF16)
    for c in range(n_chunks):
        gate = _dot(h, wu_ref[:, c * FF_CHUNK:(c + 1) * FF_CHUNK])
        val = _dot(h, wu_ref[:, d_ff + c * FF_CHUNK:d_ff + (c + 1) * FF_CHUNK])
        wk = wdw_ref[c]
        bias = bdw_ref[c]
        for j in range(FF_CHUNK // LANES):
            b = c * (FF_CHUNK // LANES) + j
            cols = slice(j * LANES, (j + 1) * LANES)
            gj = gate[:, cols]
            prev = jnp.where(first, 0.0, cbuf_ref[b, tt:tt + FF_HALO, :])
            cbuf_ref[b, 0:FF_HALO, :] = prev
            cbuf_ref[b, FF_HALO:FF_HALO + tt, :] = gj
            conv = bias[:, cols] + gj * wk[FFN_K - 1:FFN_K, cols]
            for k in range(FFN_K - 1):
                off = FF_HALO - (FFN_K - 1) + k
                conv = conv + cbuf_ref[b, off:off + tt, :] * wk[k:k + 1, cols]
            act = conv * jax.nn.sigmoid(conv) * val[:, cols]
            act_ref[:, c * FF_CHUNK + j * LANES:c * FF_CHUNK + (j + 1) * LANES] = act.astype(BF16)
    o_ref[...] = x + _dot(act_ref[...], wd_ref[...])


def _ffn(x2, mix, ng, wu, wdw, bdw, wd, seq):
    n, d = x2.shape
    tt = TOK_TILE
    nc = wd.shape[0] // FF_CHUNK

    def tok(width):
        return pl.BlockSpec((tt, width), lambda i: (i, 0))

    mix_specs = [] if mix is None else [
        tok(SB_WIDTH), tok(SG_WIDTH), _const_spec((SB_WIDTH, d)), _const_spec((SG_WIDTH, d))]
    return pl.pallas_call(
        functools.partial(_ffn_kernel, seq // tt, mix is not None),
        grid=(n // tt,),
        in_specs=[tok(d)] + mix_specs + [
            _const_spec((1, d)),
            _const_spec((d, 2 * nc * FF_CHUNK)),
            _const_spec((nc, 8, FF_CHUNK)),
            _const_spec((nc, 1, FF_CHUNK)),
            _const_spec((nc * FF_CHUNK, d)),
        ],
        out_specs=pl.BlockSpec((tt, d), lambda i: (i, 0)),
        out_shape=jax.ShapeDtypeStruct((n, d), F32),
        scratch_shapes=[
            pltpu.VMEM((nc * FF_CHUNK // LANES, tt + FF_HALO, LANES), F32),
            pltpu.VMEM((tt, nc * FF_CHUNK), BF16),
        ],
        compiler_params=_params("arbitrary"),
        name="conv_glu_ffn",
    )(x2, *(() if mix is None else mix), ng, wu, wdw, bdw, wd)


def _dft_matrices():
    n, half = DFT_N, DFT_N // 2
    p = np.arange(n)
    ang = 2.0 * np.pi * np.outer(np.arange(half), p) / n
    fwd = np.concatenate([np.cos(ang), -np.sin(ang)], axis=0)
    fwd[half] = np.cos(np.pi * p)
    inv = np.concatenate([np.cos(ang), -np.sin(ang)], axis=0).T * (2.0 / n)
    inv[:, 0] = 1.0 / n
    inv[:, half] = np.cos(np.pi * p) / n
    return fwd, inv


def _filter_spectrum(w_dw):
    k, _ = w_dw.shape
    half = DFT_N // 2
    taps = w_dw[::-1]
    ang = 2.0 * np.pi * np.outer(np.arange(half + 1), np.arange(k)) / DFT_N
    re = jnp.einsum("fj,jc->fc", jnp.asarray(np.cos(ang), F32), taps, precision=lax.Precision.HIGHEST)
    im = jnp.einsum("fj,jc->fc", jnp.asarray(-np.sin(ang), F32), taps, precision=lax.Precision.HIGHEST)
    re_nyq = jnp.concatenate([re[half:half + 1], re[1:half]], axis=0)
    return jnp.stack([re[:half], im[:half], re_nyq])


def _conformer_kernel(tiles_per_seq, x_ref, ng_ref, w1_ref, b1_ref, fwd_ref, inv_ref, spec_ref, bdw_ref,
                      lng_ref, lnb_ref, w2_ref, b2_ref, o_ref, ybuf_ref, cv_ref):
    t = pl.program_id(0)
    tt = x_ref.shape[0]
    cw = w1_ref.shape[1] // 2
    half = DFT_N // 2

    @pl.when(t == 0)
    def _():
        ybuf_ref[...] = jnp.zeros_like(ybuf_ref)

    first = (t % tiles_per_seq) == 0
    x = x_ref[...]
    h = _rms_norm(x, ng_ref[...]).astype(BF16)
    pre = _dot(h, w1_ref[...]) + b1_ref[...]
    prev = jnp.where(first, 0.0, ybuf_ref[tt:tt + CV_HALO, :])
    ybuf_ref[CV_HALO:CV_HALO + tt, :] = pre[:, :cw] * jax.nn.sigmoid(pre[:, cw:])
    ybuf_ref[0:CV_HALO, :] = prev

    hop = DFT_N - CV_HALO
    windows = [(r, hop) for r in range(0, tt - hop + 1, hop)]
    done = len(windows) * hop
    if done < tt:
        windows.append((tt + CV_HALO - DFT_N, tt - done))
    h_re, h_im, h_re_nyq = spec_ref[0], spec_ref[1], spec_ref[2]
    for r0, keep in windows:
        u = _dot(fwd_ref[...], ybuf_ref[r0:r0 + DFT_N, :].astype(BF16))
        a, b = u[:half], u[half:]
        prod = jnp.concatenate([a * h_re - b * h_im, a * h_im + b * h_re_nyq], axis=0)
        out = _dot(inv_ref[DFT_N - keep:, :], prod.astype(BF16))
        end = r0 + DFT_N - CV_HALO
        cv_ref[end - keep:end, :] = out + bdw_ref[...]

    cv = cv_ref[...]
    dc = cv - jnp.mean(cv, axis=-1, keepdims=True)
    ln = dc * lax.rsqrt(jnp.mean(dc * dc, axis=-1, keepdims=True) + EPS) * lng_ref[...] + lnb_ref[...]
    act = (ln * jax.nn.sigmoid(ln)).astype(BF16)
    o_ref[...] = x + b2_ref[...] + _dot(act, w2_ref[...])


def _conformer(x2, ng, w1, b1, fwd, inv, spec, bdw, lng, lnb, w2, b2, seq):
    n, d = x2.shape
    tt = BIG_TILE
    cw = w1.shape[1] // 2
    return pl.pallas_call(
        functools.partial(_conformer_kernel, seq // tt),
        grid=(n // tt,),
        in_specs=[
            pl.BlockSpec((tt, d), lambda i: (i, 0)),
            _const_spec((1, d)),
            _const_spec((d, 2 * cw)),
            _const_spec((1, 2 * cw)),
            _const_spec((DFT_N, DFT_N)),
            _const_spec((DFT_N, DFT_N)),
            _const_spec((3, DFT_N // 2, cw)),
            _const_spec((1, cw)),
            _const_spec((1, cw)),
            _const_spec((1, cw)),
            _const_spec((cw, d)),
            _const_spec((1, d)),
        ],
        out_specs=pl.BlockSpec((tt, d), lambda i: (i, 0)),
        out_shape=jax.ShapeDtypeStruct((n, d), F32),
        scratch_shapes=[
            pltpu.VMEM((tt + CV_HALO, cw), F32),
            pltpu.VMEM((tt, cw), F32),
        ],
        compiler_params=_params("arbitrary"),
        name="conformer",
    )(x2, ng, w1, b1, fwd, inv, spec, bdw, lng, lnb, w2, b2)


def _chunk_cols(w, width):
    lead = w.shape[:-1]
    n = w.shape[-1] // width
    return jnp.moveaxis(w.reshape(lead + (n, width)), -2, 0)


def _pad_rows(w, rows):
    return jnp.pad(w, ((0, 0), (0, rows - w.shape[1]), (0, 0)))


def kernel(x, mix_norm_g, sb_w_in, sb_q_norm_g, sb_k_norm_g, sg_z_norm_g, sg_w_spatial, sg_b_spatial,
           hyb_w_out, cv_w_pw1, cv_b_pw1, cv_w_dw, cv_b_dw, cv_ln_g, cv_ln_b, cv_w_pw2, cv_b_pw2,
           ffn_norm_g, ffn_w_up, ffn_w_dw, ffn_b_dw, ffn_w_down):
    batch, seq, d = x.shape
    depth = mix_norm_g.shape[0]
    d_ff = ffn_w_down.shape[1]
    x2 = x.reshape(batch * seq, d)

    gid = np.arange(MXU_TILE) // HEAD_DIM
    bd = jnp.asarray(gid[:, None] == gid[None, :], BF16)
    kk = np.arange(K_BLOCK)
    tri = jnp.asarray(kk[:, None] > kk[None, :], BF16)
    dft_fwd, dft_inv = (jnp.asarray(m, F32).astype(BF16) for m in _dft_matrices())

    for i in range(depth):
        j = i // 2
        ng = mix_norm_g[i].reshape(1, d)
        if i % 2 == 0:
            qg = jnp.tile(sb_q_norm_g[j], N_HEADS).reshape(1, SB_WIDTH)
            kg = jnp.tile(sb_k_norm_g[j], N_HEADS).reshape(1, SB_WIDTH)
            zg = sg_z_norm_g[j].reshape(1, SG_WIDTH)
            bs = jnp.repeat(sg_b_spatial[j].T, SG_GROUP, axis=1)
            q, kt, v, g = _mixer_in(x2, ng, sb_w_in[j].astype(BF16), qg, kg, zg, bd, sg_w_spatial[j], bs)
            a = _attention(q, kt, v, tri, batch, seq)
            w_out = hyb_w_out[j].astype(BF16)
            mix = (a, g, w_out[:SB_WIDTH], w_out[SB_WIDTH:])
        else:
            cw = cv_w_dw.shape[2]
            x2 = _conformer(
                x2, ng, cv_w_pw1[j].astype(BF16), cv_b_pw1[j].reshape(1, 2 * cw),
                dft_fwd, dft_inv, _filter_spectrum(cv_w_dw[j]),
                cv_b_dw[j].reshape(1, cw), cv_ln_g[j].reshape(1, cw), cv_ln_b[j].reshape(1, cw),
                cv_w_pw2[j].astype(BF16),
                cv_b_pw2[j].reshape(1, d), seq)
            mix = None
        x2 = _ffn(
            x2, mix, ffn_norm_g[i].reshape(1, d), ffn_w_up[i].astype(BF16),
            _pad_rows(_chunk_cols(ffn_w_dw[i], FF_CHUNK), 8),
            _chunk_cols(ffn_b_dw[i].reshape(1, d_ff), FF_CHUNK),
            ffn_w_down[i].astype(BF16), seq)
    return x2.reshape(batch, seq, d)
```

```python
import functools

import jax
import jax.numpy as jnp
import numpy as np
from jax import lax
from jax.experimental import pallas as pl
from jax.experimental.pallas import tpu as pltpu

F32 = jnp.float32
BF16 = jnp.bfloat16
EPS = 1e-6

HEAD_DIM = 64
N_HEADS = 8
SB_WIDTH = HEAD_DIM * N_HEADS
SG_GROUP = 64
SG_GROUPS = 8
SG_WIDTH = SG_GROUP * SG_GROUPS
CHUNK = 128
CONV_K = 31
FFN_K = 3

LANES = 128
TOK_TILE = 512
BIG_TILE = 1024
Q_BLOCK = 256
Q_PER_STEP = 2
MXU_TILE = 256
K_BLOCK = MXU_TILE
FF_CHUNK = 256
DFT_N = MXU_TILE
CV_HALO = 32
FF_HALO = 8
VMEM_LIMIT = 56 * 1024 * 1024
DEAD_LOG2 = -160.0


def _dot(a, b):
    return jnp.dot(a, b, preferred_element_type=F32)


def _rms_norm(x, g):
    return x * lax.rsqrt(jnp.mean(x * x, axis=-1, keepdims=True) + EPS) * g


def _gelu(t):
    return 0.5 * t * (1.0 + lax.erf(t * np.float32(0.7071067811865476)))


def _split_dot(t, m):
    hi = t.astype(BF16)
    lo = (t - hi.astype(F32)).astype(BF16)
    return _dot(hi, m) + _dot(lo, m)


def _const_spec(shape):
    nd = len(shape)
    return pl.BlockSpec(shape, lambda *_: (0,) * nd, pipeline_mode=pl.Buffered(1))


def _params(*sem):
    return pltpu.CompilerParams(dimension_semantics=sem, vmem_limit_bytes=VMEM_LIMIT)


def _mixer_in_kernel(x_ref, ng_ref, w_ref, qg_ref, kg_ref, zg_ref, bd_ref, ws_ref, bs_ref,
                     q_out, kt_out, v_out, g_out):
    tt = x_ref.shape[0]
    h = _rms_norm(x_ref[...], ng_ref[...]).astype(BF16)
    proj = _dot(h, w_ref[...])
    q = proj[:, 0:SB_WIDTH]
    k = proj[:, SB_WIDTH:2 * SB_WIDTH]
    v = proj[:, 2 * SB_WIDTH:3 * SB_WIDTH]
    u = proj[:, 3 * SB_WIDTH:3 * SB_WIDTH + SG_WIDTH]
    z = proj[:, 3 * SB_WIDTH + SG_WIDTH:]
    bd = bd_ref[...]

    def group_norm(t, g):
        t2 = (t * t).astype(BF16)
        ms = jnp.concatenate(
            [_dot(t2[:, j:j + MXU_TILE], bd) for j in range(0, t.shape[1], MXU_TILE)], axis=1)
        return t * lax.rsqrt(ms * np.float32(1.0 / HEAD_DIM) + EPS) * g

    q_out[...] = (group_norm(q, qg_ref[...]) * np.float32(HEAD_DIM ** -0.5 * np.log2(np.e))).astype(BF16)
    kn = group_norm(k, kg_ref[...])
    for j in range(tt // K_BLOCK):
        kt_out[j] = kn[j * K_BLOCK:(j + 1) * K_BLOCK, :].T.astype(BF16)
    v_out[...] = v.astype(BF16)

    ug = _gelu(u)
    zn = group_norm(_gelu(z), zg_ref[...]).astype(BF16)
    row = lax.broadcasted_iota(jnp.int32, (CHUNK, CHUNK), 0)
    col = lax.broadcasted_iota(jnp.int32, (CHUNK, CHUNK), 1)
    n_ch = tt // CHUNK
    lane = lax.broadcasted_iota(jnp.int32, (CHUNK, n_ch * LANES), 1)
    low = (lane % LANES) < SG_GROUP
    wm = [jnp.where(col <= row, ws_ref[g], 0.0).astype(BF16) for g in range(SG_GROUPS)]
    bs = bs_ref[...]
    for p in range(SG_WIDTH // LANES):
        cols = slice(p * LANES, (p + 1) * LANES)
        zp = jnp.concatenate([zn[c * CHUNK:(c + 1) * CHUNK, cols] for c in range(n_ch)], axis=1)
        rhs = jnp.concatenate(
            [jnp.where(low, zp, jnp.zeros_like(zp)), jnp.where(low, jnp.zeros_like(zp), zp)], axis=0)
        s = _dot(jnp.concatenate([wm[2 * p], wm[2 * p + 1]], axis=1), rhs)
        for c in range(n_ch):
            rows = slice(c * CHUNK, (c + 1) * CHUNK)
            g_out[rows, cols] = (ug[rows, cols] * (s[:, c * LANES:(c + 1) * LANES] + bs[:, cols])).astype(BF16)


def _mixer_in(x2, ng, w_in, qg, kg, zg, bd, ws, bs):
    n, d = x2.shape
    tt = BIG_TILE
    in_w = w_in.shape[1]
    return pl.pallas_call(
        _mixer_in_kernel,
        grid=(n // tt,),
        in_specs=[
            pl.BlockSpec((tt, d), lambda i: (i, 0)),
            _const_spec((1, d)),
            _const_spec((d, in_w)),
            _const_spec((1, SB_WIDTH)),
            _const_spec((1, SB_WIDTH)),
            _const_spec((1, SG_WIDTH)),
            _const_spec((MXU_TILE, MXU_TILE)),
            _const_spec((SG_GROUPS, CHUNK, CHUNK)),
            _const_spec((CHUNK, SG_WIDTH)),
        ],
        out_specs=[
            pl.BlockSpec((tt, SB_WIDTH), lambda i: (i, 0)),
            pl.BlockSpec((tt // K_BLOCK, SB_WIDTH, K_BLOCK), lambda i: (i, 0, 0)),
            pl.BlockSpec((tt, SB_WIDTH), lambda i: (i, 0)),
            pl.BlockSpec((tt, SG_WIDTH), lambda i: (i, 0)),
        ],
        out_shape=[
            jax.ShapeDtypeStruct((n, SB_WIDTH), BF16),
            jax.ShapeDtypeStruct((n // K_BLOCK, SB_WIDTH, K_BLOCK), BF16),
            jax.ShapeDtypeStruct((n, SB_WIDTH), BF16),
            jax.ShapeDtypeStruct((n, SG_WIDTH), BF16),
        ],
        compiler_params=_params("parallel"),
        name="mixer_in",
    )(x2, ng, w_in, qg, kg, zg, bd, ws, bs)


def _attn_kernel(q_ref, kt_ref, v_ref, tri_ref, o_ref, qm_ref, acc_ref, car_ref):
    for sub in range(Q_PER_STEP):
        rows = slice(sub * Q_BLOCK, (sub + 1) * Q_BLOCK)
        _attn_query_block(pl.program_id(1) * Q_PER_STEP + sub, q_ref.at[rows], kt_ref, v_ref, tri_ref,
                          o_ref.at[rows], qm_ref, acc_ref, car_ref)


def _attn_query_block(i, q_ref, kt_ref, v_ref, tri_ref, o_ref, qm_ref, acc_ref, car_ref):
    lane = lax.broadcasted_iota(jnp.int32, (Q_BLOCK, LANES), 1)
    row = lax.broadcasted_iota(jnp.int32, (Q_BLOCK, K_BLOCK), 0)
    col = lax.broadcasted_iota(jnp.int32, (Q_BLOCK, K_BLOCK), 1)
    causal = col < row
    low = lane < HEAD_DIM

    for h in range(N_HEADS):
        qp = q_ref[:, (h // 2) * LANES:(h // 2 + 1) * LANES]
        keep = low if h % 2 == 0 else jnp.logical_not(low)
        qm_ref[h] = jnp.where(keep, qp, jnp.zeros_like(qp))

    def block(kb, diag):
        k0 = pl.multiple_of(kb * K_BLOCK, K_BLOCK)
        grp = [slice((h // 2) * LANES, (h // 2 + 1) * LANES) for h in range(N_HEADS)]
        def qk(h):
            return _dot(qm_ref[h], kt_ref[kb, grp[h], :])

        zs = {0: qk(0), 1: qk(1)}
        args = []
        for h in range(N_HEADS):
            if h + 2 < N_HEADS:
                zs[h + 2] = qk(h + 2)
            z = zs.pop(h)
            neg_abs = lax.bitcast_convert_type(
                lax.bitcast_convert_type(z, jnp.uint32) | jnp.uint32(0x80000000), F32)
            sp = jnp.log2(1.0 + jnp.exp2(neg_abs))
            ls = jnp.minimum(z, 0.0) - sp
            lk = ls - z
            if diag:
                lk = jnp.where(causal, lk, 0.0)
            tail = _dot(lk.astype(BF16), tri_ref[...])
            row_total = jnp.broadcast_to(jnp.sum(lk, axis=1, keepdims=True), (Q_BLOCK, LANES))
            if diag:
                args.append(ls + tail)
                car_ref[h] = row_total
            else:
                car = car_ref[h]
                args.append(ls + tail + jnp.concatenate([car, car], axis=1))
                car_ref[h] = car + row_total
        for h in range(N_HEADS):
            w = jnp.exp2(args[h])
            if diag:
                w = jnp.where(causal, w, 0.0)
            pv = _dot(w.astype(BF16), v_ref[pl.ds(k0, K_BLOCK), grp[h]])
            if diag:
                acc_ref[h] = pv
            else:
                acc_ref[h] += pv

    def max_carry():
        m = car_ref[0]
        for h in range(1, N_HEADS):
            m = jnp.maximum(m, car_ref[h])
        return jnp.max(m)

    block(i, True)

    def cond(state):
        kb, mx = state
        return jnp.logical_and(kb >= 0, mx > DEAD_LOG2)

    def body(state):
        kb, _ = state
        block(kb, False)
        return kb - 1, max_carry()

    lax.while_loop(cond, body, (i - 1, jnp.float32(0.0)))

    for p in range(N_HEADS // 2):
        o_ref[:, p * LANES:(p + 1) * LANES] = jnp.where(
            low, acc_ref[2 * p], acc_ref[2 * p + 1]).astype(o_ref.dtype)


def _attention(q, kt, v, tri, batch, seq):
    qs = Q_BLOCK * Q_PER_STEP
    nq = seq // qs
    nk = seq // K_BLOCK
    return pl.pallas_call(
        _attn_kernel,
        grid=(batch, nq),
        in_specs=[
            pl.BlockSpec((qs, SB_WIDTH), lambda b, i: (b * nq + i, 0)),
            pl.BlockSpec((nk, SB_WIDTH, K_BLOCK), lambda b, i: (b, 0, 0), pipeline_mode=pl.Buffered(1)),
            pl.BlockSpec((seq, SB_WIDTH), lambda b, i: (b, 0), pipeline_mode=pl.Buffered(1)),
            _const_spec((K_BLOCK, K_BLOCK)),
        ],
        out_specs=pl.BlockSpec((qs, SB_WIDTH), lambda b, i: (b * nq + i, 0)),
        out_shape=jax.ShapeDtypeStruct((batch * seq, SB_WIDTH), BF16),
        scratch_shapes=[
            pltpu.VMEM((N_HEADS, Q_BLOCK, LANES), BF16),
            pltpu.VMEM((N_HEADS, Q_BLOCK, LANES), F32),
            pltpu.VMEM((N_HEADS, Q_BLOCK, LANES), F32),
        ],
        compiler_params=_params("parallel", "parallel"),
        name="sb_attention",
    )(q, kt, v, tri)


def _ffn_kernel(tiles_per_seq, has_mix, *refs):
    if has_mix:
        x_ref, a_ref, g_ref, wa_ref, wo_ref = refs[:5]
        refs = (x_ref,) + refs[5:]
    x_ref, ng_ref, wu_ref, wdw_ref, bdw_ref, wd_ref, o_ref, cbuf_ref, act_ref = refs
    t = pl.program_id(0)
    tt = x_ref.shape[0]
    d_ff = wd_ref.shape[0]
    n_chunks = d_ff // FF_CHUNK

    @pl.when(t == 0)
    def _():
        cbuf_ref[...] = jnp.zeros_like(cbuf_ref)

    first = (t % tiles_per_seq) == 0
    x = x_ref[...]
    if has_mix:
        x = x + _dot(a_ref[...], wa_ref[...]) + _dot(g_ref[...], wo_ref[...])
    h = _rms_norm(x, ng_ref[...]).astype(BF16)
    for c in range(n_chunks):
        gate = _dot(h, wu_ref[:, c * FF_CHUNK:(c + 1) * FF_CHUNK])
        val = _dot(h, wu_ref[:, d_ff + c * FF_CHUNK:d_ff + (c + 1) * FF_CHUNK])
        wk = wdw_ref[c]
        bias = bdw_ref[c]
        for j in range(FF_CHUNK // LANES):
            b = c * (FF_CHUNK // LANES) + j
            cols = slice(j * LANES, (j + 1) * LANES)
            gj = gate[:, cols]
            prev = jnp.where(first, 0.0, cbuf_ref[b, tt:tt + FF_HALO, :])
            cbuf_ref[b, 0:FF_HALO, :] = prev
            cbuf_ref[b, FF_HALO:FF_HALO + tt, :] = gj
            conv = bias[:, cols] + gj * wk[FFN_K - 1:FFN_K, cols]
            for k in range(FFN_K - 1):
                off = FF_HALO - (FFN_K - 1) + k
                conv = conv + cbuf_ref[b, off:off + tt, :] * wk[k:k + 1, cols]
            act = conv * jax.nn.sigmoid(conv) * val[:, cols]
            act_ref[:, c * FF_CHUNK + j * LANES:c * FF_CHUNK + (j + 1) * LANES] = act.astype(BF16)
    o_ref[...] = x + _dot(act_ref[...], wd_ref[...])


def _ffn(x2, mix, ng, wu, wdw, bdw, wd, seq):
    n, d = x2.shape
    tt = TOK_TILE
    nc = wd.shape[0] // FF_CHUNK

    def tok(width):
        return pl.BlockSpec((tt, width), lambda i: (i, 0))

    mix_specs = [] if mix is None else [
        tok(SB_WIDTH), tok(SG_WIDTH), _const_spec((SB_WIDTH, d)), _const_spec((SG_WIDTH, d))]
    return pl.pallas_call(
        functools.partial(_ffn_kernel, seq // tt, mix is not None),
        grid=(n // tt,),
        in_specs=[tok(d)] + mix_specs + [
            _const_spec((1, d)),
            _const_spec((d, 2 * nc * FF_CHUNK)),
            _const_spec((nc, 8, FF_CHUNK)),
            _const_spec((nc, 1, FF_CHUNK)),
            _const_spec((nc * FF_CHUNK, d)),
        ],
        out_specs=pl.BlockSpec((tt, d), lambda i: (i, 0)),
        out_shape=jax.ShapeDtypeStruct((n, d), F32),
        scratch_shapes=[
            pltpu.VMEM((nc * FF_CHUNK // LANES, tt + FF_HALO, LANES), F32),
            pltpu.VMEM((tt, nc * FF_CHUNK), BF16),
        ],
        compiler_params=_params("arbitrary"),
        name="conv_glu_ffn",
    )(x2, *(() if mix is None else mix), ng, wu, wdw, bdw, wd)


def _dft_matrices():
    n, half = DFT_N, DFT_N // 2
    p = np.arange(n)
    ang = 2.0 * np.pi * np.outer(np.arange(half), p) / n
    fwd = np.concatenate([np.cos(ang), -np.sin(ang)], axis=0)
    fwd[half] = np.cos(np.pi * p)
    inv = np.concatenate([np.cos(ang), -np.sin(ang)], axis=0).T * (2.0 / n)
    inv[:, 0] = 1.0 / n
    inv[:, half] = np.cos(np.pi * p) / n
    return fwd, inv


def _filter_spectrum(w_dw):
    k, _ = w_dw.shape
    half = DFT_N // 2
    taps = w_dw[::-1]
    ang = 2.0 * np.pi * np.outer(np.arange(half + 1), np.arange(k)) / DFT_N
    re = jnp.einsum("fj,jc->fc", jnp.asarray(np.cos(ang), F32), taps, precision=lax.Precision.HIGHEST)
    im = jnp.einsum("fj,jc->fc", jnp.asarray(-np.sin(ang), F32), taps, precision=lax.Precision.HIGHEST)
    re_nyq = jnp.concatenate([re[half:half + 1], re[1:half]], axis=0)
    return jnp.stack([re[:half], im[:half], re_nyq])


def _conformer_kernel(tiles_per_seq, x_ref, ng_ref, w1_ref, b1_ref, fwd_ref, inv_ref, spec_ref, bdw_ref,
                      lng_ref, lnb_ref, w2_ref, b2_ref, o_ref, ybuf_ref, cv_ref):
    t = pl.program_id(0)
    tt = x_ref.shape[0]
    cw = w1_ref.shape[1] // 2
    half = DFT_N // 2

    @pl.when(t == 0)
    def _():
        ybuf_ref[...] = jnp.zeros_like(ybuf_ref)

    first = (t % tiles_per_seq) == 0
    x = x_ref[...]
    h = _rms_norm(x, ng_ref[...]).astype(BF16)
    pre = _dot(h, w1_ref[...]) + b1_ref[...]
    prev = jnp.where(first, 0.0, ybuf_ref[tt:tt + CV_HALO, :])
    ybuf_ref[CV_HALO:CV_HALO + tt, :] = pre[:, :cw] * jax.nn.sigmoid(pre[:, cw:])
    ybuf_ref[0:CV_HALO, :] = prev

    hop = DFT_N - CV_HALO
    windows = [(r, hop) for r in range(0, tt - hop + 1, hop)]
    done = len(windows) * hop
    if done < tt:
        windows.append((tt + CV_HALO - DFT_N, tt - done))
    h_re, h_im, h_re_nyq = spec_ref[0], spec_ref[1], spec_ref[2]
    for r0, keep in windows:
        u = _dot(fwd_ref[...], ybuf_ref[r0:r0 + DFT_N, :].astype(BF16))
        a, b = u[:half], u[half:]
        prod = jnp.concatenate([a * h_re - b * h_im, a * h_im + b * h_re_nyq], axis=0)
        out = _dot(inv_ref[DFT_N - keep:, :], prod.astype(BF16))
        end = r0 + DFT_N - CV_HALO
        cv_ref[end - keep:end, :] = out + bdw_ref[...]

    cv = cv_ref[...]
    dc = cv - jnp.mean(cv, axis=-1, keepdims=True)
    ln = dc * lax.rsqrt(jnp.mean(dc * dc, axis=-1, keepdims=True) + EPS) * lng_ref[...] + lnb_ref[...]
    act = (ln * jax.nn.sigmoid(ln)).astype(BF16)
    o_ref[...] = x + b2_ref[...] + _dot(act, w2_ref[...])


def _conformer(x2, ng, w1, b1, fwd, inv, spec, bdw, lng, lnb, w2, b2, seq):
    n, d = x2.shape
    tt = BIG_TILE
    cw = w1.shape[1] // 2
    return pl.pallas_call(
        functools.partial(_conformer_kernel, seq // tt),
        grid=(n // tt,),
        in_specs=[
            pl.BlockSpec((tt, d), lambda i: (i, 0)),
            _const_spec((1, d)),
            _const_spec((d, 2 * cw)),
            _const_spec((1, 2 * cw)),
            _const_spec((DFT_N, DFT_N)),
            _const_spec((DFT_N, DFT_N)),
            _const_spec((3, DFT_N // 2, cw)),
            _const_spec((1, cw)),
            _const_spec((1, cw)),
            _const_spec((1, cw)),
            _const_spec((cw, d)),
            _const_spec((1, d)),
        ],
        out_specs=pl.BlockSpec((tt, d), lambda i: (i, 0)),
        out_shape=jax.ShapeDtypeStruct((n, d), F32),
        scratch_shapes=[
            pltpu.VMEM((tt + CV_HALO, cw), F32),
            pltpu.VMEM((tt, cw), F32),
        ],
        compiler_params=_params("arbitrary"),
        name="conformer",
    )(x2, ng, w1, b1, fwd, inv, spec, bdw, lng, lnb, w2, b2)


def _chunk_cols(w, width):
    lead = w.shape[:-1]
    n = w.shape[-1] // width
    return jnp.moveaxis(w.reshape(lead + (n, width)), -2, 0)


def _pad_rows(w, rows):
    return jnp.pad(w, ((0, 0), (0, rows - w.shape[1]), (0, 0)))


def kernel(x, mix_norm_g, sb_w_in, sb_q_norm_g, sb_k_norm_g, sg_z_norm_g, sg_w_spatial, sg_b_spatial,
           hyb_w_out, cv_w_pw1, cv_b_pw1, cv_w_dw, cv_b_dw, cv_ln_g, cv_ln_b, cv_w_pw2, cv_b_pw2,
           ffn_norm_g, ffn_w_up, ffn_w_dw, ffn_b_dw, ffn_w_down):
    batch, seq, d = x.shape
    depth = mix_norm_g.shape[0]
    d_ff = ffn_w_down.shape[1]
    x2 = x.reshape(batch * seq, d)

    gid = np.arange(MXU_TILE) // HEAD_DIM
    bd = jnp.asarray(gid[:, None] == gid[None, :], BF16)
    kk = np.arange(K_BLOCK)
    tri = jnp.asarray(kk[:, None] > kk[None, :], BF16)
    dft_fwd, dft_inv = (jnp.asarray(m, F32).astype(BF16) for m in _dft_matrices())

    for i in range(depth):
        j = i // 2
        ng = mix_norm_g[i].reshape(1, d)
        if i % 2 == 0:
            qg = jnp.tile(sb_q_norm_g[j], N_HEADS).reshape(1, SB_WIDTH)
            kg = jnp.tile(sb_k_norm_g[j], N_HEADS).reshape(1, SB_WIDTH)
            zg = sg_z_norm_g[j].reshape(1, SG_WIDTH)
            bs = jnp.repeat(sg_b_spatial[j].T, SG_GROUP, axis=1)
            q, kt, v, g = _mixer_in(x2, ng, sb_w_in[j].astype(BF16), qg, kg, zg, bd, sg_w_spatial[j], bs)
            a = _attention(q, kt, v, tri, batch, seq)
            w_out = hyb_w_out[j].astype(BF16)
            mix = (a, g, w_out[:SB_WIDTH], w_out[SB_WIDTH:])
        else:
            cw = cv_w_dw.shape[2]
            x2 = _conformer(
                x2, ng, cv_w_pw1[j].astype(BF16), cv_b_pw1[j].reshape(1, 2 * cw),
                dft_fwd, dft_inv, _filter_spectrum(cv_w_dw[j]),
                cv_b_dw[j].reshape(1, cw), cv_ln_g[j].reshape(1, cw), cv_ln_b[j].reshape(1, cw),
                cv_w_pw2[j].astype(BF16),
                cv_b_pw2[j].reshape(1, d), seq)
            mix = None
        x2 = _ffn(
            x2, mix, ffn_norm_g[i].reshape(1, d), ffn_w_up[i].astype(BF16),
            _pad_rows(_chunk_cols(ffn_w_dw[i], FF_CHUNK), 8),
            _chunk_cols(ffn_b_dw[i].reshape(1, d_ff), FF_CHUNK),
            ffn_w_down[i].astype(BF16), seq)
    return x2.reshape(batch, seq, d)
```
